```python
import jax, jax.numpy as jnp
from jax import lax
import numpy as np

D_MODEL = 1024
BATCH = 16
SEQ = 2048
DEPTH = 2
DEC_BATCH = 8
DEC_SEQ = 64
PAST_LEN = 2048

CHUNK = 64
N_A_LAYERS = DEPTH // 2
N_B_LAYERS = DEPTH - N_A_LAYERS
CONV_WIDTH = 31
CONV_STATE = CONV_WIDTH - 1
N_HEADS = 16
HEAD_DIM = D_MODEL // N_HEADS
N_KV_HEADS = 4
GROUP = N_HEADS // N_KV_HEADS
D_FF = -(-8 * D_MODEL // (3 * 256)) * 256
Q_BLOCK = 128
EPS = 1e-6

kernel_name = "yoco_conformer_conv_stick_breaking_stream_step"


def rmsnorm(x, g):
    xf = x.astype(jnp.float32)
    y = xf * lax.rsqrt(jnp.mean(xf * xf, axis=-1, keepdims=True) + EPS)
    return (y * g.astype(jnp.float32)).astype(x.dtype)


def layernorm(x, g, b):
    xf = x.astype(jnp.float32)
    mu = jnp.mean(xf, axis=-1, keepdims=True)
    xc = xf - mu
    y = xc * lax.rsqrt(jnp.mean(xc * xc, axis=-1, keepdims=True) + EPS)
    return (y * g.astype(jnp.float32) + b.astype(jnp.float32)).astype(x.dtype)


def swiglu(xn, w_gu, w_down):
    g, u = jnp.split(xn @ w_gu, 2, axis=-1)
    return (jax.nn.silu(g) * u) @ w_down


def conv_module(xn, prefix, w_in, b_in, w_dw, b_dw, ln_g, ln_b, w_out, b_out):
    a, gate = jnp.split(xn @ w_in + b_in, 2, axis=-1)
    u = a * jax.nn.sigmoid(gate)
    u_ext = jnp.concatenate([prefix.astype(u.dtype), u], axis=1)
    y = lax.conv_general_dilated(
        u_ext, w_dw[:, None, :], window_strides=(1,), padding='VALID',
        dimension_numbers=('NWC', 'WIO', 'NWC'),
        feature_group_count=D_MODEL) + b_dw
    y = jax.nn.silu(layernorm(y, ln_g, ln_b))
    return y @ w_out + b_out, u_ext[:, -CONV_STATE:]


def stick_breaking_block(q, k, v, q_pos, k_pos):
    z = jnp.einsum('bgrqd,bgsd->bgrqs', q, k,
                   preferred_element_type=jnp.float32) * (HEAD_DIM ** -0.5)
    mask = k_pos[None, :] < q_pos[:, None]
    log_beta = jax.nn.log_sigmoid(z)
    log_keep = jnp.where(mask, log_beta - z, 0.0)
    after = lax.cumsum(log_keep, axis=4, reverse=True) - log_keep
    att = jnp.where(mask, jnp.exp(log_beta + after), 0.0)
    return jnp.einsum('bgrqs,bgsd->bgrqd', att.astype(v.dtype), v)


def sb_attention(xn, k_all, v_all, q_pos, k_pos, w_q, w_o):
    bn, t, _ = xn.shape
    q = (xn @ w_q).reshape(bn, t, N_KV_HEADS, GROUP, HEAD_DIM).transpose(0, 2, 3, 1, 4)
    kt = k_all.transpose(0, 2, 1, 3)
    vt = v_all.transpose(0, 2, 1, 3)
    if t <= Q_BLOCK:
        o = stick_breaking_block(q, kt, vt, q_pos, k_pos)
    else:
        nb = t // Q_BLOCK
        qb = q.reshape(bn, N_KV_HEADS, GROUP, nb, Q_BLOCK, HEAD_DIM).transpose(3, 0, 1, 2, 4, 5)
        pb = q_pos.reshape(nb, Q_BLOCK)
        ob = lax.map(lambda a: stick_breaking_block(a[0], kt, vt, a[1], k_pos), (qb, pb))
        o = ob.transpose(1, 2, 3, 0, 4, 5).reshape(bn, N_KV_HEADS, GROUP, t, HEAD_DIM)
    o = o.transpose(0, 3, 1, 2, 4).reshape(bn, t, N_HEADS * HEAD_DIM)
    return o @ w_o


def trunk(x, conv_prefix, k_past, v_past,
          a_norm_g, conv_w_in, conv_b_in, conv_w_dw, conv_b_dw, conv_ln_g, conv_ln_b,
          conv_w_out, conv_b_out, kv_norm_g, w_kv, b_norm_g, w_q, w_o,
          ffn_norm_g, ffn_w_gu, ffn_w_down, final_norm_g):
    bn, t, _ = x.shape
    h = x
    conv_states = []
    k_new = v_new = k_all = v_all = q_pos = k_pos = None
    for l in range(DEPTH):
        if l < N_A_LAYERS:
            mix, st = conv_module(rmsnorm(h, a_norm_g[l]), conv_prefix[l],
                                  conv_w_in[l], conv_b_in[l], conv_w_dw[l], conv_b_dw[l],
                                  conv_ln_g[l], conv_ln_b[l], conv_w_out[l], conv_b_out[l])
            h = h + mix
            conv_states.append(st)
        else:
            if l == N_A_LAYERS:
                k_new, v_new = jnp.split(rmsnorm(h, kv_norm_g) @ w_kv, 2, axis=-1)
                k_new = k_new.reshape(bn, t, N_KV_HEADS, HEAD_DIM)
                v_new = v_new.reshape(bn, t, N_KV_HEADS, HEAD_DIM)
                if k_past is None:
                    past = 0
                    k_all, v_all = k_new, v_new
                else:
                    past = k_past.shape[1]
                    k_all = jnp.concatenate([k_past.astype(k_new.dtype), k_new], axis=1)
                    v_all = jnp.concatenate([v_past.astype(v_new.dtype), v_new], axis=1)
                q_pos = past + jnp.arange(t, dtype=jnp.int32)
                k_pos = jnp.arange(k_all.shape[1], dtype=jnp.int32)
            j = l - N_A_LAYERS
            h = h + sb_attention(rmsnorm(h, b_norm_g[j]), k_all, v_all, q_pos, k_pos, w_q[j], w_o[j])
        h = h + swiglu(rmsnorm(h, ffn_norm_g[l]), ffn_w_gu[l], ffn_w_down[l])
    return rmsnorm(h, final_norm_g), jnp.stack(conv_states), k_new, v_new


def setup_inputs(seed: int = 0) -> dict:
    key = jax.random.key(seed)
    ks = jax.random.split(key, 24)
    f32 = jnp.float32
    D, HD, KVD = D_MODEL, N_HEADS * HEAD_DIM, N_KV_HEADS * HEAD_DIM

    def nrm(k, shape, scale):
        return jax.random.normal(k, shape, f32) * scale

    def gain(k, shape):
        return 1.0 + 0.05 * jax.random.normal(k, shape, f32)

    return {
        "x_prompt": nrm(ks[0], (BATCH, SEQ, D), 1.0),
        "x_sample": nrm(ks[1], (DEC_BATCH, DEC_SEQ, D), 1.0),
        "state_conv": nrm(ks[2], (N_A_LAYERS, DEC_BATCH, CONV_STATE, D), 0.5),
        "cache_k": nrm(ks[3], (DEC_BATCH, PAST_LEN, N_KV_HEADS, HEAD_DIM), 1.0),
        "cache_v": nrm(ks[4], (DEC_BATCH, PAST_LEN, N_KV_HEADS, HEAD_DIM), 1.0),
        "a_norm_g": gain(ks[5], (N_A_LAYERS, D)),
        "conv_w_in": nrm(ks[6], (N_A_LAYERS, D, 2 * D), D ** -0.5),
        "conv_b_in": nrm(ks[7], (N_A_LAYERS, 2 * D), 0.02),
        "conv_w_dw": nrm(ks[8], (N_A_LAYERS, CONV_WIDTH, D), CONV_WIDTH ** -0.5),
        "conv_b_dw": nrm(ks[9], (N_A_LAYERS, D), 0.02),
        "conv_ln_g": gain(ks[10], (N_A_LAYERS, D)),
        "conv_ln_b": nrm(ks[11], (N_A_LAYERS, D), 0.02),
        "conv_w_out": nrm(ks[12], (N_A_LAYERS, D, D), D ** -0.5),
        "conv_b_out": nrm(ks[13], (N_A_LAYERS, D), 0.02),
        "kv_norm_g": gain(ks[14], (D,)),
        "w_kv": nrm(ks[15], (D, 2 * KVD), D ** -0.5),
        "b_norm_g": gain(ks[16], (N_B_LAYERS, D)),
        "w_q": nrm(ks[17], (N_B_LAYERS, D, HD), D ** -0.5),
        "w_o": nrm(ks[18], (N_B_LAYERS, HD, D), HD ** -0.5),
        "ffn_norm_g": gain(ks[19], (DEPTH, D)),
        "ffn_w_gu": nrm(ks[20], (DEPTH, D, 2 * D_FF), D ** -0.5),
        "ffn_w_down": nrm(ks[21], (DEPTH, D_FF, D), D_FF ** -0.5),
        "final_norm_g": gain(ks[22], (D,)),
    }


def reference(x_prompt, x_sample, state_conv, cache_k, cache_v,
              a_norm_g, conv_w_in, conv_b_in, conv_w_dw, conv_b_dw, conv_ln_g, conv_ln_b,
              conv_w_out, conv_b_out, kv_norm_g, w_kv, b_norm_g, w_q, w_o,
              ffn_norm_g, ffn_w_gu, ffn_w_down, final_norm_g):
    weights = (a_norm_g, conv_w_in, conv_b_in, conv_w_dw, conv_b_dw, conv_ln_g, conv_ln_b,
               conv_w_out, conv_b_out, kv_norm_g, w_kv, b_norm_g, w_q, w_o,
               ffn_norm_g, ffn_w_gu, ffn_w_down, final_norm_g)
    zero_prefix = jnp.zeros((N_A_LAYERS, x_prompt.shape[0], CONV_STATE, D_MODEL), x_prompt.dtype)
    y_prompt, conv_state_prompt, k_prompt, v_prompt = trunk(
        x_prompt, zero_prefix, None, None, *weights)
    y_sample, conv_state_sample, k_sample, v_sample = trunk(
        x_sample, state_conv, cache_k, cache_v, *weights)
    return (y_prompt, y_sample, conv_state_prompt, k_prompt, v_prompt,
            conv_state_sample, k_sample, v_sample)
```

```python
import functools

import jax
import jax.numpy as jnp
from jax import lax
from jax.experimental import pallas as pl
from jax.experimental.pallas import tpu as pltpu

F32 = jnp.float32
BF16 = jnp.bfloat16

EPS = 1e-6
CONV_WIDTH = 31
CONV_STATE = CONV_WIDTH - 1
N_HEADS = 16
N_KV_HEADS = 4
GROUP = N_HEADS // N_KV_HEADS
HEAD_DIM = 64
KV_DIM = N_KV_HEADS * HEAD_DIM

V7X_SUBLANES = 8
V7X_LANES = 128
V7X_MXU_DIM = 256
V7X_VMEM_LIMIT_BYTES = 56 * 1024 * 1024

TOKEN_TILE = 512
KEY_BLOCK = V7X_MXU_DIM
CONV_HALO = 32
CONV_ROWS = 64
CONV_LANES = 256


def _compiler_params(semantics):
    return pltpu.CompilerParams(dimension_semantics=semantics,
                                vmem_limit_bytes=V7X_VMEM_LIMIT_BYTES)


def _resident(shape):
    return pl.BlockSpec(shape, lambda *_: (0,) * len(shape), pipeline_mode=pl.Buffered(1))


def _rms(x, g):
    return x * lax.rsqrt(jnp.mean(x * x, axis=-1, keepdims=True) + EPS) * g


def _mm(a, b):
    return jnp.dot(a, b, preferred_element_type=F32)


def _mm_nt(a, b):
    return lax.dot_general(a, b, (((1,), (1,)), ((), ())), preferred_element_type=F32)


def _conv_mixer_kernel(x_ref, pre_ref, ag_ref, win_ref, bin_ref, wdw_ref, bdw_ref,
                       lng_ref, lnb_ref, wout_ref, bout_ref,
                       h_ref, st_ref, e_ref, y_ref, *, tt, d):
    t = pl.program_id(1)
    x = x_ref[0]
    xn = _rms(x, ag_ref[...]).astype(BF16)
    au = _mm(xn, win_ref[...]) + bin_ref[...]
    u = au[:, :d] * jax.nn.sigmoid(au[:, d:])

    @pl.when(t == 0)
    def _():
        e_ref[CONV_HALO - CONV_STATE:CONV_HALO, :] = pre_ref[0]

    @pl.when(t > 0)
    def _():
        e_ref[0:CONV_HALO, :] = e_ref[tt:tt + CONV_HALO, :]

    e_ref[CONV_HALO:CONV_HALO + tt, :] = u

    for r0 in range(0, tt, CONV_ROWS):
        for c in range(d // CONV_LANES):
            lanes = slice(c * CONV_LANES, (c + 1) * CONV_LANES)
            acc = jnp.broadcast_to(bdw_ref[:, lanes], (CONV_ROWS, CONV_LANES))
            for k in range(CONV_WIDTH):
                lo = r0 + k + CONV_HALO - CONV_STATE
                acc = acc + wdw_ref[k:k + 1, lanes] * e_ref[lo:lo + CONV_ROWS, lanes]
            y_ref[r0:r0 + CONV_ROWS, lanes] = acc

    y = y_ref[...]
    mu = jnp.mean(y, axis=-1, keepdims=True)
    yc = y - mu
    yn = yc * lax.rsqrt(jnp.mean(yc * yc, axis=-1, keepdims=True) + EPS)
    yn = yn * lng_ref[...] + lnb_ref[...]
    act = (yn * jax.nn.sigmoid(yn)).astype(BF16)
    h_ref[0] = x + _mm(act, wout_ref[...]) + bout_ref[...]

    @pl.when(t == pl.num_programs(1) - 1)
    def _():
        st_ref[0] = e_ref[CONV_HALO + tt - CONV_STATE:CONV_HALO + tt, :]


def _conv_mixer(x, prefix, w):
    b, t, d = x.shape
    tt = min(TOKEN_TILE, t)
    assert t % tt == 0 and tt % CONV_ROWS == 0 and tt >= CONV_HALO and d % CONV_LANES == 0
    row = lambda n: _resident((1, n))
    return pl.pallas_call(
        functools.partial(_conv_mixer_kernel, tt=tt, d=d),
        grid=(b, t // tt),
        in_specs=[
            pl.BlockSpec((1, tt, d), lambda i, j: (i, j, 0)),
            pl.BlockSpec((1, CONV_STATE, d), lambda i, j: (i, 0, 0)),
            row(d), _resident((d, 2 * d)), row(2 * d), _resident((CONV_WIDTH, d)), row(d),
            row(d), row(d), _resident((d, d)), row(d),
        ],
        out_specs=[
            pl.BlockSpec((1, tt, d), lambda i, j: (i, j, 0)),
            pl.BlockSpec((1, CONV_STATE, d), lambda i, j: (i, 0, 0)),
        ],
        out_shape=[jax.ShapeDtypeStruct((b, t, d), F32),
                   jax.ShapeDtypeStruct((b, CONV_STATE, d), F32)],
        scratch_shapes=[pltpu.VMEM((CONV_HALO + tt, d), F32), pltpu.VMEM((tt, d), F32)],
        compiler_params=_compiler_params(("arbitrary", "arbitrary")),
        name="conv_mixer",
    )(x, prefix, w["a_norm_g"], w["conv_w_in"], w["conv_b_in"], w["conv_w_dw"], w["conv_b_dw"],
      w["conv_ln_g"], w["conv_ln_b"], w["conv_w_out"], w["conv_b_out"])


def _ffn_kernel(*refs, d_ff, ff_chunk, has_attn, has_final):
    refs = list(refs)
    h_ref = refs.pop(0)
    if has_attn:
        o_ref, wo_ref = refs.pop(0), refs.pop(0)
    g_ref, wgu_ref, wd_ref = refs.pop(0), refs.pop(0), refs.pop(0)
    if has_final:
        fg_ref = refs.pop(0)
    out_ref, = refs

    h = h_ref[...]
    if has_attn:
        h = h + _mm(o_ref[...], wo_ref[...])
    xn = _rms(h, g_ref[...]).astype(BF16)
    acc = h
    for c in range(d_ff // ff_chunk):
        lo = c * ff_chunk
        gate = _mm(xn, wgu_ref[:, lo:lo + ff_chunk])
        up = _mm(xn, wgu_ref[:, d_ff + lo:d_ff + lo + ff_chunk])
        mid = (gate * jax.nn.sigmoid(gate) * up).astype(BF16)
        acc = acc + _mm(mid, wd_ref[lo:lo + ff_chunk, :])
    if has_final:
        acc = _rms(acc, fg_ref[...])
    out_ref[...] = acc


def _ffn(h, norm_g, w_gu, w_down, attn=None, final_g=None):
    n, d = h.shape
    d_ff = w_down.shape[0]
    tm = min(TOKEN_TILE, n)
    ff_chunk = d_ff // 2 if (d_ff // 2) % V7X_LANES == 0 else d_ff
    assert n % tm == 0 and d_ff % ff_chunk == 0
    tile = lambda width: pl.BlockSpec((tm, width), lambda i: (i, 0))
    args, specs = [h], [tile(d)]
    if attn is not None:
        o, w_o = attn
        args += [o, w_o]
        specs += [tile(o.shape[1]), _resident(w_o.shape)]
    args += [norm_g, w_gu, w_down]
    specs += [_resident((1, d)), _resident(w_gu.shape), _resident(w_down.shape)]
    if final_g is not None:
        args.append(final_g)
        specs.append(_resident((1, d)))
    return pl.pallas_call(
        functools.partial(_ffn_kernel, d_ff=d_ff, ff_chunk=ff_chunk,
                          has_attn=attn is not None, has_final=final_g is not None),
        grid=(n // tm,),
        in_specs=specs,
        out_specs=tile(d),
        out_shape=jax.ShapeDtypeStruct((n, d), F32),
        compiler_params=_compiler_params(("arbitrary",)),
        name="ffn_attn" if attn is not None else "ffn",
    )(*args)


def _qkv_kernel(h_ref, kvg_ref, bg_ref, wkv_ref, wvt_ref, wqt_ref,
                k_ref, v_ref, kb_ref, vt_ref, qt_ref, *, kb):
    h = h_ref[0]
    hn = h * lax.rsqrt(jnp.mean(h * h, axis=-1, keepdims=True) + EPS)
    xkv = (hn * kvg_ref[...]).astype(BF16)
    xq = (hn * bg_ref[...]).astype(BF16)
    kv = _mm(xkv, wkv_ref[...])
    k = kv[:, :KV_DIM]
    k_ref[0] = k
    v_ref[0] = kv[:, KV_DIM:]
    kb_ref[0] = k.astype(BF16)
    vt = _mm_nt(wvt_ref[...], xkv).astype(BF16)
    for g in range(N_KV_HEADS):
        for j in range(vt.shape[1] // kb):
            vt_ref[0, g, j] = vt[g * HEAD_DIM:(g + 1) * HEAD_DIM, j * kb:(j + 1) * kb]
    qt_ref[0] = (_mm_nt(wqt_ref[...], xq) * (HEAD_DIM ** -0.5)).astype(BF16)


def _qkv(h, kv_norm_g, b_norm_g, w_kv, w_vt, w_qt, kb):
    b, t, d = h.shape
    tm = min(TOKEN_TILE, t)
    assert t % tm == 0 and tm % kb == 0
    hd = w_qt.shape[0]
    return pl.pallas_call(
        functools.partial(_qkv_kernel, kb=kb),
        grid=(b, t // tm),
        in_specs=[
            pl.BlockSpec((1, tm, d), lambda i, j: (i, j, 0)),
            _resident((1, d)), _resident((1, d)),
            _resident(w_kv.shape), _resident(w_vt.shape), _resident(w_qt.shape),
        ],
        out_specs=[
            pl.BlockSpec((1, tm, KV_DIM), lambda i, j: (i, j, 0)),
            pl.BlockSpec((1, tm, KV_DIM), lambda i, j: (i, j, 0)),
            pl.BlockSpec((1, tm, KV_DIM), lambda i, j: (i, j, 0)),
            pl.BlockSpec((1, N_KV_HEADS, tm // kb, HEAD_DIM, kb), lambda i, j: (i, 0, j, 0, 0)),
            pl.BlockSpec((1, hd, tm), lambda i, j: (i, 0, j)),
        ],
        out_shape=[
            jax.ShapeDtypeStruct((b, t, KV_DIM), F32),
            jax.ShapeDtypeStruct((b, t, KV_DIM), F32),
            jax.ShapeDtypeStruct((b, t, KV_DIM), BF16),
            jax.ShapeDtypeStruct((b, N_KV_HEADS, t // kb, HEAD_DIM, kb), BF16),
            jax.ShapeDtypeStruct((b, hd, t), BF16),
        ],
        compiler_params=_compiler_params(("arbitrary", "arbitrary")),
        name="qkv",
    )(h, kv_norm_g, b_norm_g, w_kv, w_vt, w_qt)


def _softplus(z):
    return jnp.maximum(z, 0.0) + jnp.log(1.0 + jnp.exp(-jnp.abs(z)))


def _attn_kernel(qt_ref, ki_ref, vi_ref, kd_ref, vd_ref, o_ref,
                 qz_ref, tri_ref, carry_ref, acc_ref, *, tq, n_int_static):
    g = pl.program_id(1)
    n_int = pl.program_id(2) if n_int_static is None else n_int_static

    qz_ref[...] = jnp.zeros_like(qz_ref)
    row0 = pl.multiple_of(g * HEAD_DIM, HEAD_DIM)
    for r in range(GROUP):
        qz_ref[r, pl.ds(row0, HEAD_DIM), :] = qt_ref[0, r * HEAD_DIM:(r + 1) * HEAD_DIM, :]

    si = lax.broadcasted_iota(jnp.int32, (KEY_BLOCK, KEY_BLOCK), 0)
    sj = lax.broadcasted_iota(jnp.int32, (KEY_BLOCK, KEY_BLOCK), 1)
    tri_ref[...] = jnp.where(sj > si, -1.0, 0.0).astype(BF16)

    kd = kd_ref[0]
    mask = (lax.broadcasted_iota(jnp.int32, (tq, tq), 0)
            < lax.broadcasted_iota(jnp.int32, (tq, tq), 1))
    for r in range(GROUP):
        z = _mm(kd, qz_ref[r])
        sp = jnp.where(mask, _softplus(z), 0.0)
        after = _mm(tri_ref[0:tq, 0:tq], sp.astype(BF16))
        att = jnp.where(mask, jnp.exp(z - sp + after), 0.0)
        acc_ref[r * HEAD_DIM:(r + 1) * HEAD_DIM, :] = _mm(vd_ref[0, 0, 0], att.astype(BF16))
        carry_ref[r:r + 1, :] = after[0:1, :] - sp[0:1, :]

    def block(i, c):
        j = n_int - 1 - i
        kb = ki_ref[0, pl.ds(pl.multiple_of(j * KEY_BLOCK, KEY_BLOCK), KEY_BLOCK), :]
        vb = vi_ref[0, 0, j]
        for r in range(GROUP):
            z = _mm(kb, qz_ref[r])
            sp = _softplus(z)
            after = _mm(tri_ref[...], sp.astype(BF16)) + carry_ref[r:r + 1, :]
            att = jnp.exp(z - sp + after)
            acc_ref[r * HEAD_DIM:(r + 1) * HEAD_DIM, :] += _mm(vb, att.astype(BF16))
            carry_ref[r:r + 1, :] = after[0:1, :] - sp[0:1, :]
        return c

    lax.fori_loop(0, n_int, block, 0)

    o_ref[0] = acc_ref[...].T.astype(o_ref.dtype)


def _attention(qt, k_int, vt_int, k_diag, vt_diag, tq, n_int_static):
    b, hd_all, t = qt.shape
    nq = t // tq
    s_int = k_int.shape[1]
    n_blk = vt_int.shape[2]
    assert s_int == n_blk * KEY_BLOCK and vt_diag.shape[-1] == tq
    gw = GROUP * HEAD_DIM
    return pl.pallas_call(
        functools.partial(_attn_kernel, tq=tq, n_int_static=n_int_static),
        grid=(b, N_KV_HEADS, nq),
        in_specs=[
            pl.BlockSpec((1, gw, tq), lambda i, g, q: (i, g, q)),
            pl.BlockSpec((1, s_int, KV_DIM), lambda i, g, q: (i, 0, 0)),
            pl.BlockSpec((1, 1, n_blk, HEAD_DIM, KEY_BLOCK), lambda i, g, q: (i, g, 0, 0, 0)),
            pl.BlockSpec((1, tq, KV_DIM), lambda i, g, q: (i, q, 0)),
            pl.BlockSpec((1, 1, 1, HEAD_DIM, tq), lambda i, g, q: (i, g, q, 0, 0)),
        ],
        out_specs=pl.BlockSpec((1, tq, gw), lambda i, g, q: (i, q, g)),
        out_shape=jax.ShapeDtypeStruct((b, t, hd_all), BF16),
        scratch_shapes=[
            pltpu.VMEM((GROUP, KV_DIM, tq), BF16),
            pltpu.VMEM((KEY_BLOCK, KEY_BLOCK), BF16),
            pltpu.VMEM((V7X_SUBLANES, tq), F32),
            pltpu.VMEM((gw, tq), F32),
        ],
        compiler_params=_compiler_params(("arbitrary", "arbitrary", "arbitrary")),
        name="sb_attention",
    )(qt, k_int, vt_int, k_diag, vt_diag)


def _trunk(x, conv_prefix, past, w):
    b, t, d = x.shape
    h, conv_state = _conv_mixer(x, conv_prefix, w)
    h = _ffn(h.reshape(b * t, d), w["ffn_norm_g"][0], w["ffn_w_gu"][0], w["ffn_w_down"][0])
    tq = min(KEY_BLOCK, t)
    k, v, kb, vt, qt = _qkv(h.reshape(b, t, d), w["kv_norm_g"], w["b_norm_g"],
                            w["w_kv"], w["w_vt"], w["w_qt"], tq)
    if past is None:
        o = _attention(qt, kb, vt, kb, vt, tq, None)
    else:
        k_past, vt_past = past
        o = _attention(qt, k_past, vt_past, kb, vt, tq, k_past.shape[1] // KEY_BLOCK)
    y = _ffn(h, w["ffn_norm_g"][1], w["ffn_w_gu"][1], w["ffn_w_down"][1],
             attn=(o.reshape(b * t, -1), w["w_o"]), final_g=w["final_norm_g"])
    shape4 = (b, t, N_KV_HEADS, HEAD_DIM)
    return y.reshape(b, t, d), conv_state[None], k.reshape(shape4), v.reshape(shape4)


def kernel(x_prompt, x_sample, state_conv, cache_k, cache_v, a_norm_g, conv_w_in, conv_b_in, conv_w_dw, conv_b_dw, conv_ln_g, conv_ln_b, conv_w_out, conv_b_out, kv_norm_g, w_kv, b_norm_g, w_q, w_o, ffn_norm_g, ffn_w_gu, ffn_w_down, final_norm_g):
    assert a_norm_g.shape[0] == 1 and b_norm_g.shape[0] == 1 and ffn_norm_g.shape[0] == 2
    d = x_prompt.shape[-1]
    row = lambda a: a.reshape(1, -1)
    w = {
        "a_norm_g": row(a_norm_g[0]),
        "conv_w_in": conv_w_in[0].astype(BF16), "conv_b_in": row(conv_b_in[0]),
        "conv_w_dw": conv_w_dw[0], "conv_b_dw": row(conv_b_dw[0]),
        "conv_ln_g": row(conv_ln_g[0]), "conv_ln_b": row(conv_ln_b[0]),
        "conv_w_out": conv_w_out[0].astype(BF16), "conv_b_out": row(conv_b_out[0]),
        "kv_norm_g": row(kv_norm_g), "b_norm_g": row(b_norm_g[0]),
        "w_kv": w_kv.astype(BF16),
        "w_vt": w_kv[:, KV_DIM:].T.astype(BF16),
        "w_qt": w_q[0].T.astype(BF16),
        "w_o": w_o[0].astype(BF16),
        "ffn_norm_g": [row(ffn_norm_g[0]), row(ffn_norm_g[1])],
        "ffn_w_gu": [ffn_w_gu[0].astype(BF16), ffn_w_gu[1].astype(BF16)],
        "ffn_w_down": [ffn_w_down[0].astype(BF16), ffn_w_down[1].astype(BF16)],
        "final_norm_g": row(final_norm_g),
    }

    bp = x_prompt.shape[0]
    zero_prefix = jnp.zeros((bp, CONV_STATE, d), x_prompt.dtype)
    y_p, cs_p, k_p, v_p = _trunk(x_prompt, zero_prefix, None, w)

    bs, past_len = cache_k.shape[0], cache_k.shape[1]
    assert past_len % KEY_BLOCK == 0
    k_past = cache_k.reshape(bs, past_len, KV_DIM).astype(BF16)
    vt_past = (cache_v.reshape(bs, past_len // KEY_BLOCK, KEY_BLOCK, N_KV_HEADS, HEAD_DIM)
               .transpose(0, 3, 1, 4, 2).astype(BF16))
    y_s, cs_s, k_s, v_s = _trunk(x_sample, state_conv[0], (k_past, vt_past), w)
    return (y_p, y_s, cs_p, k_p, v_p, cs_s, k_s, v_s)
```

```python
import functools
import math

import jax
import jax.numpy as jnp
from jax import lax
from jax.experimental import pallas as pl
from jax.experimental.pallas import tpu as pltpu

F32 = jnp.float32
BF16 = jnp.bfloat16

EPS = 1e-6
CONV_WIDTH = 31
CONV_STATE = CONV_WIDTH - 1
N_HEADS = 16
N_KV_HEADS = 4
GROUP = N_HEADS // N_KV_HEADS
HEAD_DIM = 64
KV_DIM = N_KV_HEADS * HEAD_DIM
LOG2E = math.log2(math.e)

V7X_SUBLANES = 8
V7X_LANES = 128
V7X_MXU_DIM = 256
V7X_VMEM_LIMIT_BYTES = 56 * 1024 * 1024

TOKEN_TILE = 512
KEY_BLOCK = V7X_MXU_DIM
CONV_HALO = 32
CONV_ROWS = 64
CONV_LANES = 256


def _compiler_params(semantics):
    return pltpu.CompilerParams(dimension_semantics=semantics,
                                vmem_limit_bytes=V7X_VMEM_LIMIT_BYTES)


def _resident(shape):
    return pl.BlockSpec(shape, lambda *_: (0,) * len(shape), pipeline_mode=pl.Buffered(1))


def _rms(x, g):
    return x * lax.rsqrt(jnp.mean(x * x, axis=-1, keepdims=True) + EPS) * g


def _mm(a, b):
    return jnp.dot(a, b, preferred_element_type=F32)


def _mm_nt(a, b):
    return lax.dot_general(a, b, (((1,), (1,)), ((), ())), preferred_element_type=F32)


def _conv_mixer_kernel(x_ref, pre_ref, ag_ref, win_ref, bin_ref, wdw_ref, bdw_ref,
                       lng_ref, lnb_ref, wout_ref, bout_ref,
                       h_ref, st_ref, e_ref, y_ref, *, tt, d):
    t = pl.program_id(1)
    x = x_ref[0]
    xn = _rms(x, ag_ref[...]).astype(BF16)
    au = _mm(xn, win_ref[...]) + bin_ref[...]
    u = au[:, :d] * jax.nn.sigmoid(au[:, d:])

    @pl.when(t == 0)
    def _():
        e_ref[CONV_HALO - CONV_STATE:CONV_HALO, :] = pre_ref[0]

    @pl.when(t > 0)
    def _():
        e_ref[0:CONV_HALO, :] = e_ref[tt:tt + CONV_HALO, :]

    e_ref[CONV_HALO:CONV_HALO + tt, :] = u

    first_off = CONV_HALO - CONV_STATE
    last_off = first_off + CONV_WIDTH - 1
    for r0 in range(0, tt, CONV_ROWS):
        for c in range(d // CONV_LANES):
            lanes = slice(c * CONV_LANES, (c + 1) * CONV_LANES)
            acc = jnp.broadcast_to(bdw_ref[:, lanes], (CONV_ROWS, CONV_LANES))
            for s in range(V7X_SUBLANES):
                rows = CONV_ROWS + (V7X_SUBLANES if s else 0)
                part = None
                for o in range(s, last_off + 1, V7X_SUBLANES):
                    if o < first_off:
                        continue
                    k = o - first_off
                    lo = r0 + o - s
                    term = wdw_ref[k:k + 1, lanes] * e_ref[lo:lo + rows, lanes]
                    part = term if part is None else part + term
                acc = acc + part[s:s + CONV_ROWS]
            y_ref[r0:r0 + CONV_ROWS, lanes] = acc

    y = y_ref[...]
    mu = jnp.mean(y, axis=-1, keepdims=True)
    yc = y - mu
    yn = yc * lax.rsqrt(jnp.mean(yc * yc, axis=-1, keepdims=True) + EPS)
    yn = yn * lng_ref[...] + lnb_ref[...]
    act = (yn * jax.nn.sigmoid(yn)).astype(BF16)
    h_ref[0] = x + _mm(act, wout_ref[...]) + bout_ref[...]

    @pl.when(t == pl.num_programs(1) - 1)
    def _():
        st_ref[0] = e_ref[CONV_HALO + tt - CONV_STATE:CONV_HALO + tt, :]


def _conv_mixer(x, prefix, w):
    b, t, d = x.shape
    tt = min(TOKEN_TILE, t)
    assert t % tt == 0 and tt % CONV_ROWS == 0 and tt >= CONV_HALO and d % CONV_LANES == 0
    row = lambda n: _resident((1, n))
    return pl.pallas_call(
        functools.partial(_conv_mixer_kernel, tt=tt, d=d),
        grid=(b, t // tt),
        in_specs=[
            pl.BlockSpec((1, tt, d), lambda i, j: (i, j, 0)),
            pl.BlockSpec((1, CONV_STATE, d), lambda i, j: (i, 0, 0)),
            row(d), _resident((d, 2 * d)), row(2 * d), _resident((CONV_WIDTH, d)), row(d),
            row(d), row(d), _resident((d, d)), row(d),
        ],
        out_specs=[
            pl.BlockSpec((1, tt, d), lambda i, j: (i, j, 0)),
            pl.BlockSpec((1, CONV_STATE, d), lambda i, j: (i, 0, 0)),
        ],
        out_shape=[jax.ShapeDtypeStruct((b, t, d), F32),
                   jax.ShapeDtypeStruct((b, CONV_STATE, d), F32)],
        scratch_shapes=[pltpu.VMEM((CONV_HALO + tt, d), F32), pltpu.VMEM((tt, d), F32)],
        compiler_params=_compiler_params(("arbitrary", "arbitrary")),
        name="conv_mixer",
    )(x, prefix, w["a_norm_g"], w["conv_w_in"], w["conv_b_in"], w["conv_w_dw"], w["conv_b_dw"],
      w["conv_ln_g"], w["conv_ln_b"], w["conv_w_out"], w["conv_b_out"])


def _ffn_kernel(*refs, d_ff, ff_chunk, has_attn, has_final):
    refs = list(refs)
    h_ref = refs.pop(0)
    if has_attn:
        o_ref, wo_ref = refs.pop(0), refs.pop(0)
    g_ref, wgu_ref, wd_ref = refs.pop(0), refs.pop(0), refs.pop(0)
    if has_final:
        fg_ref = refs.pop(0)
    out_ref, = refs

    h = h_ref[...]
    if has_attn:
        h = h + _mm(o_ref[...], wo_ref[...])
    xn = _rms(h, g_ref[...]).astype(BF16)
    acc = h
    for c in range(d_ff // ff_chunk):
        lo = c * ff_chunk
        gate = _mm(xn, wgu_ref[:, lo:lo + ff_chunk])
        up = _mm(xn, wgu_ref[:, d_ff + lo:d_ff + lo + ff_chunk])
        mid = (gate * jax.nn.sigmoid(gate) * up).astype(BF16)
        acc = acc + _mm(mid, wd_ref[lo:lo + ff_chunk, :])
    if has_final:
        acc = _rms(acc, fg_ref[...])
    out_ref[...] = acc


def _ffn(h, norm_g, w_gu, w_down, attn=None, final_g=None):
    n, d = h.shape
    d_ff = w_down.shape[0]
    tm = min(TOKEN_TILE, n)
    ff_chunk = d_ff // 2 if (d_ff // 2) % V7X_LANES == 0 else d_ff
    assert n % tm == 0 and d_ff % ff_chunk == 0
    tile = lambda width: pl.BlockSpec((tm, width), lambda i: (i, 0))
    args, specs = [h], [tile(d)]
    if attn is not None:
        o, w_o = attn
        args += [o, w_o]
        specs += [tile(o.shape[1]), _resident(w_o.shape)]
    args += [norm_g, w_gu, w_down]
    specs += [_resident((1, d)), _resident(w_gu.shape), _resident(w_down.shape)]
    if final_g is not None:
        args.append(final_g)
        specs.append(_resident((1, d)))
    return pl.pallas_call(
        functools.partial(_ffn_kernel, d_ff=d_ff, ff_chunk=ff_chunk,
                          has_attn=attn is not None, has_final=final_g is not None),
        grid=(n // tm,),
        in_specs=specs,
        out_specs=tile(d),
        out_shape=jax.ShapeDtypeStruct((n, d), F32),
        compiler_params=_compiler_params(("arbitrary",)),
        name="ffn_attn" if attn is not None else "ffn",
    )(*args)


def _qkv_kernel(h_ref, kvg_ref, bg_ref, wkv_ref, wvt_ref, wqt_ref,
                k_ref, v_ref, kb_ref, vt_ref, qt_ref, *, kb):
    h = h_ref[0]
    hn = h * lax.rsqrt(jnp.mean(h * h, axis=-1, keepdims=True) + EPS)
    xkv = (hn * kvg_ref[...]).astype(BF16)
    xq = (hn * bg_ref[...]).astype(BF16)
    kv = _mm(xkv, wkv_ref[...])
    k = kv[:, :KV_DIM]
    k_ref[0] = k
    v_ref[0] = kv[:, KV_DIM:]
    kb_ref[0] = k.astype(BF16)
    vt = _mm_nt(wvt_ref[...], xkv).astype(BF16)
    for g in range(N_KV_HEADS):
        for j in range(vt.shape[1] // kb):
            vt_ref[0, g, j] = vt[g * HEAD_DIM:(g + 1) * HEAD_DIM, j * kb:(j + 1) * kb]
    qt_ref[0] = (_mm_nt(wqt_ref[...], xq) * (HEAD_DIM ** -0.5 * LOG2E)).astype(BF16)


def _qkv(h, kv_norm_g, b_norm_g, w_kv, w_vt, w_qt, kb):
    b, t, d = h.shape
    tm = min(TOKEN_TILE, t)
    assert t % tm == 0 and tm % kb == 0
    hd = w_qt.shape[0]
    return pl.pallas_call(
        functools.partial(_qkv_kernel, kb=kb),
        grid=(b, t // tm),
        in_specs=[
            pl.BlockSpec((1, tm, d), lambda i, j: (i, j, 0)),
            _resident((1, d)), _resident((1, d)),
            _resident(w_kv.shape), _resident(w_vt.shape), _resident(w_qt.shape),
        ],
        out_specs=[
            pl.BlockSpec((1, tm, KV_DIM), lambda i, j: (i, j, 0)),
            pl.BlockSpec((1, tm, KV_DIM), lambda i, j: (i, j, 0)),
            pl.BlockSpec((1, tm, KV_DIM), lambda i, j: (i, j, 0)),
            pl.BlockSpec((1, N_KV_HEADS, tm // kb, HEAD_DIM, kb), lambda i, j: (i, 0, j, 0, 0)),
            pl.BlockSpec((1, hd, tm), lambda i, j: (i, 0, j)),
        ],
        out_shape=[
            jax.ShapeDtypeStruct((b, t, KV_DIM), F32),
            jax.ShapeDtypeStruct((b, t, KV_DIM), F32),
            jax.ShapeDtypeStruct((b, t, KV_DIM), BF16),
            jax.ShapeDtypeStruct((b, N_KV_HEADS, t // kb, HEAD_DIM, kb), BF16),
            jax.ShapeDtypeStruct((b, hd, t), BF16),
        ],
        compiler_params=_compiler_params(("arbitrary", "arbitrary")),
        name="qkv",
    )(h, kv_norm_g, b_norm_g, w_kv, w_vt, w_qt)


EXP2_CLAMP = 126.0
MASKED_LOGIT = -1e30


def _softplus2(z):
    return jnp.maximum(z, jnp.log(1.0 + jnp.exp2(jnp.minimum(z, EXP2_CLAMP))) * LOG2E)


def _attn_kernel(q_ref, ki_ref, vi_ref, kd_ref, vd_ref, o_ref,
                 qz_ref, tri_ref, carry_ref, acc_ref,
                 *, lane_groups, tq, n_int_static, transpose_out):
    n_int = pl.program_id(1) if n_int_static is None else n_int_static
    groups = range(len(lane_groups))

    qz_ref[...] = jnp.zeros_like(qz_ref)
    for i, (g, qb) in enumerate(lane_groups):
        qz_ref[i, g * HEAD_DIM:(g + 1) * HEAD_DIM, :] = q_ref[0, qb * HEAD_DIM:(qb + 1) * HEAD_DIM, :]

    si = lax.broadcasted_iota(jnp.int32, (KEY_BLOCK, KEY_BLOCK), 0)
    sj = lax.broadcasted_iota(jnp.int32, (KEY_BLOCK, KEY_BLOCK), 1)
    tri_ref[...] = jnp.where(sj > si, -1.0, 0.0).astype(BF16)

    def key_block(kb, vb, mask, first):
        zs = [_mm(kb, qz_ref[i]) for i in groups]
        sps = [_softplus2(z) for z in zs]
        if mask is None:
            ds = [z - sp for z, sp in zip(zs, sps)]
        else:
            sps = [jnp.where(mask, sp, 0.0) for sp in sps]
            ds = [jnp.where(mask, z - sp, MASKED_LOGIT) for z, sp in zip(zs, sps)]
        spbs = [sp.astype(BF16) for sp in sps]
        afters = [_mm(tri_ref[...], spb) for spb in spbs]
        if not first:
            afters = [a + carry_ref[i:i + 1, :] for i, a in zip(groups, afters)]
        atts = [jnp.exp2(d + a).astype(BF16) for d, a in zip(ds, afters)]
        for i, (g, _) in enumerate(lane_groups):
            rows = slice(i * HEAD_DIM, (i + 1) * HEAD_DIM)
            pv = _mm(vb(g), atts[i])
            acc_ref[rows, :] = pv if first else acc_ref[rows, :] + pv
            carry_ref[i:i + 1, :] = afters[i][0:1, :] - spbs[i][0:1, :].astype(F32)

    mask = (lax.broadcasted_iota(jnp.int32, (KEY_BLOCK, KEY_BLOCK), 0)
            < lax.broadcasted_iota(jnp.int32, (KEY_BLOCK, KEY_BLOCK), 1) % tq)
    key_block(kd_ref[0], lambda g: vd_ref[0, g, 0], mask, True)

    def block(i, c):
        j = n_int - 1 - i
        kb = ki_ref[0, pl.ds(pl.multiple_of(j * KEY_BLOCK, KEY_BLOCK), KEY_BLOCK), :]
        key_block(kb, lambda g: vi_ref[0, g, j], None, False)
        return c

    lax.fori_loop(0, n_int, block, 0)

    if transpose_out:
        o_ref[0] = acc_ref[...].T.astype(o_ref.dtype)
    else:
        o_ref[0] = acc_ref[...].astype(o_ref.dtype)


def _attention(q, k_int, vt_int, k_diag, vt_diag, *, lane_groups, tq, n_int_static, transpose_out):
    b, q_rows, lanes_all = q.shape
    nq = lanes_all // KEY_BLOCK
    s_int = k_int.shape[1]
    n_blk = vt_int.shape[2]
    width = len(lane_groups) * HEAD_DIM
    assert s_int == n_blk * KEY_BLOCK and k_diag.shape[1] == nq * KEY_BLOCK
    assert vt_diag.shape[2:] == (nq, HEAD_DIM, KEY_BLOCK)
    if transpose_out:
        out_spec = pl.BlockSpec((1, KEY_BLOCK, width), lambda i, t: (i, t, 0))
        out_shape = jax.ShapeDtypeStruct((b, nq * KEY_BLOCK, width), BF16)
    else:
        out_spec = pl.BlockSpec((1, width, KEY_BLOCK), lambda i, t: (i, 0, t))
        out_shape = jax.ShapeDtypeStruct((b, width, nq * KEY_BLOCK), BF16)
    return pl.pallas_call(
        functools.partial(_attn_kernel, lane_groups=lane_groups, tq=tq,
                          n_int_static=n_int_static, transpose_out=transpose_out),
        grid=(b, nq),
        in_specs=[
            pl.BlockSpec((1, q_rows, KEY_BLOCK), lambda i, t: (i, 0, t)),
            pl.BlockSpec((1, s_int, KV_DIM), lambda i, t: (i, 0, 0)),
            pl.BlockSpec((1, N_KV_HEADS, n_blk, HEAD_DIM, KEY_BLOCK), lambda i, t: (i, 0, 0, 0, 0)),
            pl.BlockSpec((1, KEY_BLOCK, KV_DIM), lambda i, t: (i, t, 0)),
            pl.BlockSpec((1, N_KV_HEADS, 1, HEAD_DIM, KEY_BLOCK), lambda i, t: (i, 0, t, 0, 0)),
        ],
        out_specs=out_spec,
        out_shape=out_shape,
        scratch_shapes=[
            pltpu.VMEM((len(lane_groups), KV_DIM, KEY_BLOCK), BF16),
            pltpu.VMEM((KEY_BLOCK, KEY_BLOCK), BF16),
            pltpu.VMEM((len(lane_groups), KEY_BLOCK), F32),
            pltpu.VMEM((width, KEY_BLOCK), F32),
        ],
        compiler_params=_compiler_params(("arbitrary", "arbitrary")),
        name="sb_attention",
    )(q, k_int, vt_int, k_diag, vt_diag)


def _trunk(x, conv_prefix, past, w):
    b, t, d = x.shape
    h, conv_state = _conv_mixer(x, conv_prefix, w)
    h = _ffn(h.reshape(b * t, d), w["ffn_norm_g"][0], w["ffn_w_gu"][0], w["ffn_w_down"][0])
    tq = min(KEY_BLOCK, t)
    k, v, kb, vt, qt = _qkv(h.reshape(b, t, d), w["kv_norm_g"], w["b_norm_g"],
                            w["w_kv"], w["w_vt"], w["w_qt"], tq)
    if past is None:
        assert tq == KEY_BLOCK
        heads = tuple((hh // GROUP, hh) for hh in range(N_HEADS))
        o = _attention(qt, kb, vt, kb, vt, lane_groups=heads, tq=tq,
                       n_int_static=None, transpose_out=True)
        o = o.reshape(b * t, -1)
    else:
        assert t * GROUP == KEY_BLOCK
        k_past, vt_past = past
        q = (qt.reshape(b, N_KV_HEADS, GROUP, HEAD_DIM, t).transpose(0, 1, 3, 2, 4)
             .reshape(b, KV_DIM, KEY_BLOCK))
        pad = KEY_BLOCK - t
        k_new = jnp.pad(kb, ((0, 0), (0, pad), (0, 0)))
        vt_new = jnp.pad(vt, ((0, 0), (0, 0), (0, 0), (0, 0), (0, pad)))
        o = _attention(q, k_past, vt_past, k_new, vt_new,
                       lane_groups=tuple((g, g) for g in range(N_KV_HEADS)), tq=t,
                       n_int_static=k_past.shape[1] // KEY_BLOCK, transpose_out=False)
        o = (o.reshape(b, N_KV_HEADS, HEAD_DIM, GROUP, t).transpose(0, 4, 1, 3, 2)
             .reshape(b * t, -1))
    y = _ffn(h, w["ffn_norm_g"][1], w["ffn_w_gu"][1], w["ffn_w_down"][1],
             attn=(o, w["w_o"]), final_g=w["final_norm_g"])
    shape4 = (b, t, N_KV_HEADS, HEAD_DIM)
    return y.reshape(b, t, d), conv_state[None], k.reshape(shape4), v.reshape(shape4)


def kernel(x_prompt, x_sample, state_conv, cache_k, cache_v, a_norm_g, conv_w_in, conv_b_in, conv_w_dw, conv_b_dw, conv_ln_g, conv_ln_b, conv_w_out, conv_b_out, kv_norm_g, w_kv, b_norm_g, w_q, w_o, ffn_norm_g, ffn_w_gu, ffn_w_down, final_norm_g):
    assert a_norm_g.shape[0] == 1 and b_norm_g.shape[0] == 1 and ffn_norm_g.shape[0] == 2
    d = x_prompt.shape[-1]
    row = lambda a: a.reshape(1, -1)
    w = {
        "a_norm_g": row(a_norm_g[0]),
        "conv_w_in": conv_w_in[0].astype(BF16), "conv_b_in": row(conv_b_in[0]),
        "conv_w_dw": conv_w_dw[0], "conv_b_dw": row(conv_b_dw[0]),
        "conv_ln_g": row(conv_ln_g[0]), "conv_ln_b": row(conv_ln_b[0]),
        "conv_w_out": conv_w_out[0].astype(BF16), "conv_b_out": row(conv_b_out[0]),
        "kv_norm_g": row(kv_norm_g), "b_norm_g": row(b_norm_g[0]),
        "w_kv": w_kv.astype(BF16),
        "w_vt": w_kv[:, KV_DIM:].T.astype(BF16),
        "w_qt": w_q[0].T.astype(BF16),
        "w_o": w_o[0].astype(BF16),
        "ffn_norm_g": [row(ffn_norm_g[0]), row(ffn_norm_g[1])],
        "ffn_w_gu": [ffn_w_gu[0].astype(BF16), ffn_w_gu[1].astype(BF16)],
        "ffn_w_down": [ffn_w_down[0].astype(BF16), ffn_w_down[1].astype(BF16)],
        "final_norm_g": row(final_norm_g),
    }

    bp = x_prompt.shape[0]
    zero_prefix = jnp.zeros((bp, CONV_STATE, d), x_prompt.dtype)
    y_p, cs_p, k_p, v_p = _trunk(x_prompt, zero_prefix, None, w)

    bs, past_len = cache_k.shape[0], cache_k.shape[1]
    assert past_len % KEY_BLOCK == 0
    k_past = cache_k.reshape(bs, past_len, KV_DIM).astype(BF16)
    vt_past = (cache_v.reshape(bs, past_len // KEY_BLOCK, KEY_BLOCK, N_KV_HEADS, HEAD_DIM)
               .transpose(0, 3, 1, 4, 2).astype(BF16))
    y_s, cs_s, k_s, v_s = _trunk(x_sample, state_conv[0], (k_past, vt_past), w)
    return (y_p, y_s, cs_p, k_p, v_p, cs_s, k_s, v_s)
```

```python
import functools
import math

import jax
import jax.numpy as jnp
from jax import lax
from jax.experimental import pallas as pl
from jax.experimental.pallas import tpu as pltpu

F32 = jnp.float32
BF16 = jnp.bfloat16

EPS = 1e-6
CONV_WIDTH = 31
CONV_STATE = CONV_WIDTH - 1
N_HEADS = 16
N_KV_HEADS = 4
GROUP = N_HEADS // N_KV_HEADS
HEAD_DIM = 64
KV_DIM = N_KV_HEADS * HEAD_DIM
LOG2E = math.log2(math.e)

V7X_SUBLANES = 8
V7X_LANES = 128
V7X_MXU_DIM = 256
V7X_VMEM_LIMIT_BYTES = 56 * 1024 * 1024

TOKEN_TILE = 512
KEY_BLOCK = V7X_MXU_DIM
CONV_HALO = 32
CONV_ROWS = 64
CONV_LANES = 256


def _compiler_params(semantics):
    return pltpu.CompilerParams(dimension_semantics=semantics,
                                vmem_limit_bytes=V7X_VMEM_LIMIT_BYTES)


def _resident(shape):
    return pl.BlockSpec(shape, lambda *_: (0,) * len(shape), pipeline_mode=pl.Buffered(1))


def _rms(x, g):
    return x * lax.rsqrt(jnp.mean(x * x, axis=-1, keepdims=True) + EPS) * g


def _mm(a, b):
    return jnp.dot(a, b, preferred_element_type=F32)


def _mm_nt(a, b):
    return lax.dot_general(a, b, (((1,), (1,)), ((), ())), preferred_element_type=F32)


def _conv_mixer_kernel(x_ref, pre_ref, ag_ref, win_ref, bin_ref, wdw_ref, bdw_ref,
                       lng_ref, lnb_ref, wout_ref, bout_ref,
                       h_ref, st_ref, e_ref, y_ref, *, tt, d):
    t = pl.program_id(1)
    x = x_ref[0]
    xn = _rms(x, ag_ref[...]).astype(BF16)
    au = _mm(xn, win_ref[...]) + bin_ref[...]
    u = au[:, :d] * jax.nn.sigmoid(au[:, d:])

    @pl.when(t == 0)
    def _():
        e_ref[CONV_HALO - CONV_STATE:CONV_HALO, :] = pre_ref[0]

    @pl.when(t > 0)
    def _():
        e_ref[0:CONV_HALO, :] = e_ref[tt:tt + CONV_HALO, :]

    e_ref[CONV_HALO:CONV_HALO + tt, :] = u

    first_off = CONV_HALO - CONV_STATE
    last_off = first_off + CONV_WIDTH - 1
    for r0 in range(0, tt, CONV_ROWS):
        for c in range(d // CONV_LANES):
            lanes = slice(c * CONV_LANES, (c + 1) * CONV_LANES)
            acc = jnp.broadcast_to(bdw_ref[:, lanes], (CONV_ROWS, CONV_LANES))
            for s in range(V7X_SUBLANES):
                rows = CONV_ROWS + (V7X_SUBLANES if s else 0)
                part = None
                for o in range(s, last_off + 1, V7X_SUBLANES):
                    if o < first_off:
                        continue
                    k = o - first_off
                    lo = r0 + o - s
                    term = wdw_ref[k:k + 1, lanes] * e_ref[lo:lo + rows, lanes]
                    part = term if part is None else part + term
                acc = acc + part[s:s + CONV_ROWS]
            y_ref[r0:r0 + CONV_ROWS, lanes] = acc

    y = y_ref[...]
    mu = jnp.mean(y, axis=-1, keepdims=True)
    yc = y - mu
    yn = yc * lax.rsqrt(jnp.mean(yc * yc, axis=-1, keepdims=True) + EPS)
    yn = yn * lng_ref[...] + lnb_ref[...]
    act = (yn * jax.nn.sigmoid(yn)).astype(BF16)
    h_ref[0] = x + _mm(act, wout_ref[...]) + bout_ref[...]

    @pl.when(t == pl.num_programs(1) - 1)
    def _():
        st_ref[0] = e_ref[CONV_HALO + tt - CONV_STATE:CONV_HALO + tt, :]


def _conv_mixer(x, prefix, w):
    b, t, d = x.shape
    tt = min(TOKEN_TILE, t)
    assert t % tt == 0 and tt % CONV_ROWS == 0 and tt >= CONV_HALO and d % CONV_LANES == 0
    row = lambda n: _resident((1, n))
    return pl.pallas_call(
        functools.partial(_conv_mixer_kernel, tt=tt, d=d),
        grid=(b, t // tt),
        in_specs=[
            pl.BlockSpec((1, tt, d), lambda i, j: (i, j, 0)),
            pl.BlockSpec((1, CONV_STATE, d), lambda i, j: (i, 0, 0)),
            row(d), _resident((d, 2 * d)), row(2 * d), _resident((CONV_WIDTH, d)), row(d),
            row(d), row(d), _resident((d, d)), row(d),
        ],
        out_specs=[
            pl.BlockSpec((1, tt, d), lambda i, j: (i, j, 0)),
            pl.BlockSpec((1, CONV_STATE, d), lambda i, j: (i, 0, 0)),
        ],
        out_shape=[jax.ShapeDtypeStruct((b, t, d), F32),
                   jax.ShapeDtypeStruct((b, CONV_STATE, d), F32)],
        scratch_shapes=[pltpu.VMEM((CONV_HALO + tt, d), F32), pltpu.VMEM((tt, d), F32)],
        compiler_params=_compiler_params(("arbitrary", "arbitrary")),
        name="conv_mixer",
    )(x, prefix, w["a_norm_g"], w["conv_w_in"], w["conv_b_in"], w["conv_w_dw"], w["conv_b_dw"],
      w["conv_ln_g"], w["conv_ln_b"], w["conv_w_out"], w["conv_b_out"])


def _ffn_kernel(*refs, d_ff, ff_chunk, has_attn, has_final):
    refs = list(refs)
    h_ref = refs.pop(0)
    if has_attn:
        o_ref, wo_ref = refs.pop(0), refs.pop(0)
    g_ref, wgu_ref, wd_ref = refs.pop(0), refs.pop(0), refs.pop(0)
    if has_final:
        fg_ref = refs.pop(0)
    out_ref, = refs

    h = h_ref[...]
    if has_attn:
        h = h + _mm(o_ref[...], wo_ref[...])
    xn = _rms(h, g_ref[...]).astype(BF16)
    acc = h
    for c in range(d_ff // ff_chunk):
        lo = c * ff_chunk
        gate = _mm(xn, wgu_ref[:, lo:lo + ff_chunk])
        up = _mm(xn, wgu_ref[:, d_ff + lo:d_ff + lo + ff_chunk])
        mid = (gate * jax.nn.sigmoid(gate) * up).astype(BF16)
        acc = acc + _mm(mid, wd_ref[lo:lo + ff_chunk, :])
    if has_final:
        acc = _rms(acc, fg_ref[...])
    out_ref[...] = acc


def _ffn(h, norm_g, w_gu, w_down, attn=None, final_g=None):
    n, d = h.shape
    d_ff = w_down.shape[0]
    tm = min(TOKEN_TILE, n)
    ff_chunk = d_ff // 2 if (d_ff // 2) % V7X_LANES == 0 else d_ff
    assert n % tm == 0 and d_ff % ff_chunk == 0
    tile = lambda width: pl.BlockSpec((tm, width), lambda i: (i, 0))
    args, specs = [h], [tile(d)]
    if attn is not None:
        o, w_o = attn
        args += [o, w_o]
        specs += [tile(o.shape[1]), _resident(w_o.shape)]
    args += [norm_g, w_gu, w_down]
    specs += [_resident((1, d)), _resident(w_gu.shape), _resident(w_down.shape)]
    if final_g is not None:
        args.append(final_g)
        specs.append(_resident((1, d)))
    return pl.pallas_call(
        functools.partial(_ffn_kernel, d_ff=d_ff, ff_chunk=ff_chunk,
                          has_attn=attn is not None, has_final=final_g is not None),
        grid=(n // tm,),
        in_specs=specs,
        out_specs=tile(d),
        out_shape=jax.ShapeDtypeStruct((n, d), F32),
        compiler_params=_compiler_params(("arbitrary",)),
        name="ffn_attn" if attn is not None else "ffn",
    )(*args)


def _qkv_kernel(h_ref, kvg_ref, bg_ref, wkv_ref, wvt_ref, wqt_ref,
                k_ref, v_ref, kb_ref, vt_ref, qt_ref, *, kb):
    h = h_ref[0]
    hn = h * lax.rsqrt(jnp.mean(h * h, axis=-1, keepdims=True) + EPS)
    xkv = (hn * kvg_ref[...]).astype(BF16)
    xq = (hn * bg_ref[...]).astype(BF16)
    kv = _mm(xkv, wkv_ref[...])
    k = kv[:, :KV_DIM]
    k_ref[0] = k
    v_ref[0] = kv[:, KV_DIM:]
    kb_ref[0] = k.astype(BF16)
    vt = _mm_nt(wvt_ref[...], xkv).astype(BF16)
    for g in range(N_KV_HEADS):
        for j in range(vt.shape[1] // kb):
            vt_ref[0, g, j] = vt[g * HEAD_DIM:(g + 1) * HEAD_DIM, j * kb:(j + 1) * kb]
    qt_ref[0] = (_mm_nt(wqt_ref[...], xq) * (HEAD_DIM ** -0.5 * LOG2E)).astype(BF16)


def _qkv(h, kv_norm_g, b_norm_g, w_kv, w_vt, w_qt, kb):
    b, t, d = h.shape
    tm = min(TOKEN_TILE, t)
    assert t % tm == 0 and tm % kb == 0
    hd = w_qt.shape[0]
    return pl.pallas_call(
        functools.partial(_qkv_kernel, kb=kb),
        grid=(b, t // tm),
        in_specs=[
            pl.BlockSpec((1, tm, d), lambda i, j: (i, j, 0)),
            _resident((1, d)), _resident((1, d)),
            _resident(w_kv.shape), _resident(w_vt.shape), _resident(w_qt.shape),
        ],
        out_specs=[
            pl.BlockSpec((1, tm, KV_DIM), lambda i, j: (i, j, 0)),
            pl.BlockSpec((1, tm, KV_DIM), lambda i, j: (i, j, 0)),
            pl.BlockSpec((1, tm, KV_DIM), lambda i, j: (i, j, 0)),
            pl.BlockSpec((1, N_KV_HEADS, tm // kb, HEAD_DIM, kb), lambda i, j: (i, 0, j, 0, 0)),
            pl.BlockSpec((1, hd, tm), lambda i, j: (i, 0, j)),
        ],
        out_shape=[
            jax.ShapeDtypeStruct((b, t, KV_DIM), F32),
            jax.ShapeDtypeStruct((b, t, KV_DIM), F32),
            jax.ShapeDtypeStruct((b, t, KV_DIM), BF16),
            jax.ShapeDtypeStruct((b, N_KV_HEADS, t // kb, HEAD_DIM, kb), BF16),
            jax.ShapeDtypeStruct((b, hd, t), BF16),
        ],
        compiler_params=_compiler_params(("arbitrary", "arbitrary")),
        name="qkv",
    )(h, kv_norm_g, b_norm_g, w_kv, w_vt, w_qt)


ATTN_DOT_BATCH = 8
ATTN_TRI_LAG = 1
ATTN_PV_LAG = 2
EXP2_CLAMP = 126.0
MASKED_LOGIT = -1e30
ZERO_WEIGHT_LOG2 = -160.0


def _softplus2(z):
    return jnp.maximum(z, jnp.log(1.0 + jnp.exp2(jnp.minimum(z, EXP2_CLAMP))) * LOG2E)


def _attn_kernel(q_ref, ki_ref, vi_ref, kd_ref, vd_ref, o_ref,
                 qz_ref, tri_ref, carry_ref, acc_ref,
                 *, lane_groups, tq, n_int_static, transpose_out):
    n_int = pl.program_id(1) if n_int_static is None else n_int_static
    groups = range(len(lane_groups))

    qz_ref[...] = jnp.zeros_like(qz_ref)
    for i, (g, qb) in enumerate(lane_groups):
        qz_ref[i, g * HEAD_DIM:(g + 1) * HEAD_DIM, :] = q_ref[0, qb * HEAD_DIM:(qb + 1) * HEAD_DIM, :]

    si = lax.broadcasted_iota(jnp.int32, (KEY_BLOCK, KEY_BLOCK), 0)
    sj = lax.broadcasted_iota(jnp.int32, (KEY_BLOCK, KEY_BLOCK), 1)
    tri_ref[...] = jnp.where(sj > si, -1.0, 0.0).astype(BF16)

    def key_block(kb, vb, mask, first):
        n = len(lane_groups)
        per = min(ATTN_DOT_BATCH, n)
        assert n % per == 0
        batches = [range(p, p + per) for p in range(0, n, per)]
        zs, ds, spbs, afters = {}, {}, {}, {}
        for step in range(len(batches) + ATTN_PV_LAG):
            if step < len(batches):
                for i in batches[step]:
                    zs[i] = _mm(kb, qz_ref[i])
            if 0 <= step - ATTN_TRI_LAG < len(batches):
                for i in batches[step - ATTN_TRI_LAG]:
                    z = zs.pop(i)
                    sp = _softplus2(z)
                    if mask is None:
                        ds[i] = z - sp
                    else:
                        sp = jnp.where(mask, sp, 0.0)
                        ds[i] = jnp.where(mask, z - sp, MASKED_LOGIT)
                    spbs[i] = sp.astype(BF16)
                for i in batches[step - ATTN_TRI_LAG]:
                    afters[i] = _mm(tri_ref[...], spbs[i])
            if 0 <= step - ATTN_PV_LAG < len(batches):
                atts = {}
                for i in batches[step - ATTN_PV_LAG]:
                    if not first:
                        afters[i] = afters[i] + carry_ref[i, 0:1, :]
                    atts[i] = jnp.exp2(ds.pop(i) + afters[i]).astype(BF16)
                for i in batches[step - ATTN_PV_LAG]:
                    rows = slice(i * HEAD_DIM, (i + 1) * HEAD_DIM)
                    pv = _mm(vb(lane_groups[i][0]), atts[i])
                    acc_ref[rows, :] = pv if first else acc_ref[rows, :] + pv
                    carry_ref[i, 0:1, :] = afters.pop(i)[0:1, :] - spbs.pop(i)[0:1, :].astype(F32)

    mask = (lax.broadcasted_iota(jnp.int32, (KEY_BLOCK, KEY_BLOCK), 0)
            < lax.broadcasted_iota(jnp.int32, (KEY_BLOCK, KEY_BLOCK), 1) % tq)
    key_block(kd_ref[0], lambda g: vd_ref[0, g, 0], mask, True)

    def any_weight_left():
        worst = carry_ref[0, 0:1, :]
        for i in groups[1:]:
            worst = jnp.maximum(worst, carry_ref[i, 0:1, :])
        return jnp.max(worst) > ZERO_WEIGHT_LOG2

    def block(state):
        i, _ = state
        j = n_int - 1 - i
        kb = ki_ref[0, pl.ds(pl.multiple_of(j * KEY_BLOCK, KEY_BLOCK), KEY_BLOCK), :]
        key_block(kb, lambda g: vi_ref[0, g, j], None, False)
        return i + 1, any_weight_left()

    lax.while_loop(lambda state: jnp.logical_and(state[0] < n_int, state[1]), block,
                   (jnp.int32(0), any_weight_left()))

    if transpose_out:
        o_ref[0] = acc_ref[...].T.astype(o_ref.dtype)
    else:
        o_ref[0] = acc_ref[...].astype(o_ref.dtype)


def _attention(q, k_int, vt_int, k_diag, vt_diag, *, lane_groups, tq, n_int_static, transpose_out):
    b, q_rows, lanes_all = q.shape
    nq = lanes_all // KEY_BLOCK
    s_int = k_int.shape[1]
    n_blk = vt_int.shape[2]
    width = len(lane_groups) * HEAD_DIM
    assert s_int == n_blk * KEY_BLOCK and k_diag.shape[1] == nq * KEY_BLOCK
    assert vt_diag.shape[2:] == (nq, HEAD_DIM, KEY_BLOCK)
    if transpose_out:
        out_spec = pl.BlockSpec((1, KEY_BLOCK, width), lambda i, t: (i, t, 0))
        out_shape = jax.ShapeDtypeStruct((b, nq * KEY_BLOCK, width), BF16)
    else:
        out_spec = pl.BlockSpec((1, width, KEY_BLOCK), lambda i, t: (i, 0, t))
        out_shape = jax.ShapeDtypeStruct((b, width, nq * KEY_BLOCK), BF16)
    return pl.pallas_call(
        functools.partial(_attn_kernel, lane_groups=lane_groups, tq=tq,
                          n_int_static=n_int_static, transpose_out=transpose_out),
        grid=(b, nq),
        in_specs=[
            pl.BlockSpec((1, q_rows, KEY_BLOCK), lambda i, t: (i, 0, t)),
            pl.BlockSpec((1, s_int, KV_DIM), lambda i, t: (i, 0, 0)),
            pl.BlockSpec((1, N_KV_HEADS, n_blk, HEAD_DIM, KEY_BLOCK), lambda i, t: (i, 0, 0, 0, 0)),
            pl.BlockSpec((1, KEY_BLOCK, KV_DIM), lambda i, t: (i, t, 0)),
            pl.BlockSpec((1, N_KV_HEADS, 1, HEAD_DIM, KEY_BLOCK), lambda i, t: (i, 0, t, 0, 0)),
        ],
        out_specs=out_spec,
        out_shape=out_shape,
        scratch_shapes=[
            pltpu.VMEM((len(lane_groups), KV_DIM, KEY_BLOCK), BF16),
            pltpu.VMEM((KEY_BLOCK, KEY_BLOCK), BF16),
            pltpu.VMEM((len(lane_groups), V7X_SUBLANES, KEY_BLOCK), F32),
            pltpu.VMEM((width, KEY_BLOCK), F32),
        ],
        compiler_params=_compiler_params(("arbitrary", "arbitrary")),
        name="sb_attention",
    )(q, k_int, vt_int, k_diag, vt_diag)


def _trunk(x, conv_prefix, past, w):
    b, t, d = x.shape
    h, conv_state = _conv_mixer(x, conv_prefix, w)
    h = _ffn(h.reshape(b * t, d), w["ffn_norm_g"][0], w["ffn_w_gu"][0], w["ffn_w_down"][0])
    tq = min(KEY_BLOCK, t)
    k, v, kb, vt, qt = _qkv(h.reshape(b, t, d), w["kv_norm_g"], w["b_norm_g"],
                            w["w_kv"], w["w_vt"], w["w_qt"], tq)
    if past is None:
        assert tq == KEY_BLOCK
        heads = tuple((hh // GROUP, hh) for hh in range(N_HEADS))
        o = _attention(qt, kb, vt, kb, vt, lane_groups=heads, tq=tq,
                       n_int_static=None, transpose_out=True)
        o = o.reshape(b * t, -1)
    else:
        assert t * GROUP == KEY_BLOCK
        k_past, vt_past = past
        q = (qt.reshape(b, N_KV_HEADS, GROUP, HEAD_DIM, t).transpose(0, 1, 3, 2, 4)
             .reshape(b, KV_DIM, KEY_BLOCK))
        pad = KEY_BLOCK - t
        k_new = jnp.pad(kb, ((0, 0), (0, pad), (0, 0)))
        vt_new = jnp.pad(vt, ((0, 0), (0, 0), (0, 0), (0, 0), (0, pad)))
        o = _attention(q, k_past, vt_past, k_new, vt_new,
                       lane_groups=tuple((g, g) for g in range(N_KV_HEADS)), tq=t,
                       n_int_static=k_past.shape[1] // KEY_BLOCK, transpose_out=False)
        o = (o.reshape(b, N_KV_HEADS, HEAD_DIM, GROUP, t).transpose(0, 4, 1, 3, 2)
             .reshape(b * t, -1))
    y = _ffn(h, w["ffn_norm_g"][1], w["ffn_w_gu"][1], w["ffn_w_down"][1],
             attn=(o, w["w_o"]), final_g=w["final_norm_g"])
    shape4 = (b, t, N_KV_HEADS, HEAD_DIM)
    return y.reshape(b, t, d), conv_state[None], k.reshape(shape4), v.reshape(shape4)


def kernel(x_prompt, x_sample, state_conv, cache_k, cache_v, a_norm_g, conv_w_in, conv_b_in, conv_w_dw, conv_b_dw, conv_ln_g, conv_ln_b, conv_w_out, conv_b_out, kv_norm_g, w_kv, b_norm_g, w_q, w_o, ffn_norm_g, ffn_w_gu, ffn_w_down, final_norm_g):
    assert a_norm_g.shape[0] == 1 and b_norm_g.shape[0] == 1 and ffn_norm_g.shape[0] == 2
    d = x_prompt.shape[-1]
    row = lambda a: a.reshape(1, -1)
    w = {
        "a_norm_g": row(a_norm_g[0]),
        "conv_w_in": conv_w_in[0].astype(BF16), "conv_b_in": row(conv_b_in[0]),
        "conv_w_dw": conv_w_dw[0], "conv_b_dw": row(conv_b_dw[0]),
        "conv_ln_g": row(conv_ln_g[0]), "conv_ln_b": row(conv_ln_b[0]),
        "conv_w_out": conv_w_out[0].astype(BF16), "conv_b_out": row(conv_b_out[0]),
        "kv_norm_g": row(kv_norm_g), "b_norm_g": row(b_norm_g[0]),
        "w_kv": w_kv.astype(BF16),
        "w_vt": w_kv[:, KV_DIM:].T.astype(BF16),
        "w_qt": w_q[0].T.astype(BF16),
        "w_o": w_o[0].astype(BF16),
        "ffn_norm_g": [row(ffn_norm_g[0]), row(ffn_norm_g[1])],
        "ffn_w_gu": [ffn_w_gu[0].astype(BF16), ffn_w_gu[1].astype(BF16)],
        "ffn_w_down": [ffn_w_down[0].astype(BF16), ffn_w_down[1].astype(BF16)],
        "final_norm_g": row(final_norm_g),
    }

    bp = x_prompt.shape[0]
    zero_prefix = jnp.zeros((bp, CONV_STATE, d), x_prompt.dtype)
    y_p, cs_p, k_p, v_p = _trunk(x_prompt, zero_prefix, None, w)

    bs, past_len = cache_k.shape[0], cache_k.shape[1]
    assert past_len % KEY_BLOCK == 0
    k_past = cache_k.reshape(bs, past_len, KV_DIM).astype(BF16)
    vt_past = (cache_v.reshape(bs, past_len // KEY_BLOCK, KEY_BLOCK, N_KV_HEADS, HEAD_DIM)
               .transpose(0, 3, 1, 4, 2).astype(BF16))
    y_s, cs_s, k_s, v_s = _trunk(x_sample, state_conv[0], (k_past, vt_past), w)
    return (y_p, y_s, cs_p, k_p, v_p, cs_s, k_s, v_s)
```

```python
import functools
import math

import jax
import jax.numpy as jnp
from jax import lax
from jax.experimental import pallas as pl
from jax.experimental.pallas import tpu as pltpu

F32 = jnp.float32
BF16 = jnp.bfloat16

EPS = 1e-6
CONV_WIDTH = 31
CONV_STATE = CONV_WIDTH - 1
N_HEADS = 16
N_KV_HEADS = 4
GROUP = N_HEADS // N_KV_HEADS
HEAD_DIM = 64
KV_DIM = N_KV_HEADS * HEAD_DIM
LOG2E = math.log2(math.e)

V7X_SUBLANES = 8
V7X_LANES = 128
V7X_MXU_DIM = 256
V7X_VMEM_LIMIT_BYTES = 56 * 1024 * 1024

TOKEN_TILE = 512
KEY_BLOCK = V7X_MXU_DIM
CONV_HALO = 32
CONV_ROWS = 64
CONV_LANES = 128
GLU_LANES = 512
FFN_LANES = 256


def _compiler_params(semantics):
    return pltpu.CompilerParams(dimension_semantics=semantics,
                                vmem_limit_bytes=V7X_VMEM_LIMIT_BYTES)


def _resident(shape):
    return pl.BlockSpec(shape, lambda *_: (0,) * len(shape), pipeline_mode=pl.Buffered(1))


def _rms(x, g):
    return x * lax.rsqrt(jnp.mean(x * x, axis=-1, keepdims=True) + EPS) * g


def _mm(a, b):
    return jnp.dot(a, b, preferred_element_type=F32)


def _mm_nt(a, b):
    return lax.dot_general(a, b, (((1,), (1,)), ((), ())), preferred_element_type=F32)


def _swiglu(h, g_ref, wgu_ref, wd_ref, d_ff, ff_chunk):
    xn = _rms(h, g_ref[...]).astype(BF16)
    acc = h
    for lo in range(0, d_ff, ff_chunk):
        gate = _mm(xn, wgu_ref[:, lo:lo + ff_chunk])
        up = _mm(xn, wgu_ref[:, d_ff + lo:d_ff + lo + ff_chunk])
        mid = (gate * jax.nn.sigmoid(gate) * up).astype(BF16)
        acc = acc + _mm(mid, wd_ref[lo:lo + ff_chunk, :])
    return acc


def _layer0_kernel(x_ref, pre_ref, ag_ref, win_ref, bin_ref, wdw_ref, bdw_ref,
                   lng_ref, lnb_ref, wout_ref, bout_ref, fg_ref, wgu_ref, wd_ref,
                   h2_ref, st_ref, e_ref, y_ref, h1_ref, xf_ref, mid_ref, *, tt, d, nt, d_ff):
    s = pl.program_id(0)
    t = lax.rem(jnp.minimum(s, pl.num_programs(0) - 2), nt)
    sections = iter(pl.when(s >= -i) for i in range(3))

    @pl.when(s == 0)
    def _():
        h1_ref[...] = jnp.zeros_like(h1_ref)

    first_off = CONV_HALO - CONV_STATE
    last_off = first_off + CONV_WIDTH - 1
    out_rows = min(CONV_ROWS, tt)
    conv_chunks = [(r0, c) for r0 in range(0, tt, out_rows) for c in range(d // CONV_LANES)]

    def never(tile):
        return jnp.where(s < 0, tile, 0.0)

    def after(value, dep):
        if dep is None:
            return value
        rows = dep.shape[0]
        return jnp.concatenate([value[0:rows] + dep, value[rows:]], axis=0)

    def conv(chunks, dep=None):
        for r0, c in chunks:
            lanes = slice(c * CONV_LANES, (c + 1) * CONV_LANES)
            acc = after(jnp.broadcast_to(bdw_ref[:, lanes], (out_rows, CONV_LANES)), dep)
            for r in range(V7X_SUBLANES):
                rows = out_rows + (V7X_SUBLANES if r else 0)
                part = None
                for o in range(r, last_off + 1, V7X_SUBLANES):
                    if o < first_off:
                        continue
                    k = o - first_off
                    lo = r0 + o - r
                    wk = pltpu.repeat(wdw_ref[k * V7X_SUBLANES:(k + 1) * V7X_SUBLANES, lanes],
                                      rows // V7X_SUBLANES, axis=0)
                    term = wk * e_ref[lo:lo + rows, lanes]
                    part = term if part is None else part + term
                acc = acc + (pltpu.roll(part, rows - r, axis=0)[0:out_rows] if r else part)
            y_ref[r0:r0 + out_rows, lanes] = acc
            dep = never(acc[0:V7X_SUBLANES])
        return dep

    @next(sections)
    def _():
        xn = _rms(x_ref[0], ag_ref[...]).astype(BF16)
        e_ref[0:CONV_HALO, :] = jnp.where(t == 0, pre_ref[0], e_ref[tt:tt + CONV_HALO, :])
        for lo in range(0, d, GLU_LANES):
            a = _mm(xn, win_ref[:, lo:lo + GLU_LANES]) + bin_ref[:, lo:lo + GLU_LANES]
            g = (_mm(xn, win_ref[:, d + lo:d + lo + GLU_LANES])
                 + bin_ref[:, d + lo:d + lo + GLU_LANES])
            e_ref[CONV_HALO:CONV_HALO + tt, lo:lo + GLU_LANES] = a * jax.nn.sigmoid(g)
        xf_ref[...] = _rms(h1_ref[...], fg_ref[...]).astype(BF16)

    @next(sections)
    def _():
        n_dots = 2 * d_ff // FFN_LANES
        share = lambda i: conv_chunks[len(conv_chunks) * i // n_dots:len(conv_chunks) * (i + 1) // n_dots]
        dot_dep = conv_dep = None
        for i, lo in enumerate(range(0, d_ff, FFN_LANES)):
            conv_dep = conv(share(2 * i), conv_dep)
            gate = after(_mm(xf_ref[...], wgu_ref[:, lo:lo + FFN_LANES]), dot_dep)
            conv_dep = conv(share(2 * i + 1), conv_dep)
            up = after(_mm(xf_ref[...], wgu_ref[:, d_ff + lo:d_ff + lo + FFN_LANES]),
                       never(gate[0:V7X_SUBLANES]))
            mid_ref[:, lo:lo + FFN_LANES] = (gate * jax.nn.sigmoid(gate) * up).astype(BF16)
            dot_dep = never(up[0:V7X_SUBLANES])

    @next(sections)
    def _():
        h2_ref[...] = h1_ref[...] + _mm(mid_ref[...], wd_ref[...])
        y = y_ref[...]
        mu = jnp.mean(y, axis=-1, keepdims=True)
        yc = y - mu
        yn = yc * lax.rsqrt(jnp.mean(yc * yc, axis=-1, keepdims=True) + EPS)
        yn = yn * lng_ref[...] + lnb_ref[...]
        act = (yn * jax.nn.sigmoid(yn)).astype(BF16)
        h1_ref[...] = x_ref[0] + _mm(act, wout_ref[...]) + bout_ref[...]
        st_ref[0] = e_ref[CONV_HALO + tt - CONV_STATE:CONV_HALO + tt, :]


def _layer0(x, prefix, w):
    b, t, d = x.shape
    tt = min(TOKEN_TILE, t)
    nt = t // tt
    n_tiles = b * nt
    w_gu, w_down = w["ffn_w_gu"][0], w["ffn_w_down"][0]
    d_ff = w_down.shape[0]
    assert t % tt == 0 and tt % min(CONV_ROWS, tt) == 0 and tt >= CONV_HALO
    assert d % CONV_LANES == 0 and d % GLU_LANES == 0 and d_ff % FFN_LANES == 0
    halo = jnp.pad(prefix, ((0, 0), (CONV_HALO - CONV_STATE, 0), (0, 0)))
    w_dw8 = jnp.repeat(w["conv_w_dw"], V7X_SUBLANES, axis=0)
    mixer_tile = lambda s: jnp.minimum(s, n_tiles - 1)
    row = lambda n: _resident((1, n))
    return pl.pallas_call(
        functools.partial(_layer0_kernel, tt=tt, d=d, nt=nt, d_ff=d_ff),
        grid=(n_tiles + 1,),
        in_specs=[
            pl.BlockSpec((1, tt, d), lambda s: (mixer_tile(s) // nt, mixer_tile(s) % nt, 0)),
            pl.BlockSpec((1, CONV_HALO, d), lambda s: (mixer_tile(s) // nt, 0, 0)),
            row(d), _resident((d, 2 * d)), row(2 * d), _resident(w_dw8.shape), row(d),
            row(d), row(d), _resident((d, d)), row(d),
            row(d), _resident(w_gu.shape), _resident(w_down.shape),
        ],
        out_specs=[
            pl.BlockSpec((tt, d), lambda s: (jnp.maximum(s - 1, 0), 0)),
            pl.BlockSpec((1, CONV_STATE, d), lambda s: (mixer_tile(s) // nt, 0, 0)),
        ],
        out_shape=[jax.ShapeDtypeStruct((b * t, d), F32),
                   jax.ShapeDtypeStruct((b, CONV_STATE, d), F32)],
        scratch_shapes=[
            pltpu.VMEM((CONV_HALO + tt, d), F32),
            pltpu.VMEM((tt, d), F32),
            pltpu.VMEM((tt, d), F32),
            pltpu.VMEM((tt, d), BF16),
            pltpu.VMEM((tt, d_ff), BF16),
        ],
        compiler_params=_compiler_params(("arbitrary",)),
        name="layer0",
    )(x, halo, w["a_norm_g"], w["conv_w_in"], w["conv_b_in"], w_dw8, w["conv_b_dw"],
      w["conv_ln_g"], w["conv_ln_b"], w["conv_w_out"], w["conv_b_out"],
      w["ffn_norm_g"][0], w_gu, w_down)


def _ffn_kernel(*refs, d_ff, ff_chunk, has_attn, has_final):
    refs = list(refs)
    h_ref = refs.pop(0)
    if has_attn:
        o_ref, wo_ref = refs.pop(0), refs.pop(0)
    g_ref, wgu_ref, wd_ref = refs.pop(0), refs.pop(0), refs.pop(0)
    if has_final:
        fg_ref = refs.pop(0)
    out_ref, = refs

    h = h_ref[...]
    if has_attn:
        h = h + _mm(o_ref[...], wo_ref[...])
    acc = _swiglu(h, g_ref, wgu_ref, wd_ref, d_ff, ff_chunk)
    if has_final:
        acc = _rms(acc, fg_ref[...])
    out_ref[...] = acc


def _ffn(h, norm_g, w_gu, w_down, attn=None, final_g=None):
    n, d = h.shape
    d_ff = w_down.shape[0]
    tm = min(TOKEN_TILE, n)
    ff_chunk = d_ff // 2 if (d_ff // 2) % V7X_LANES == 0 else d_ff
    assert n % tm == 0 and d_ff % ff_chunk == 0
    tile = lambda width: pl.BlockSpec((tm, width), lambda i: (i, 0))
    args, specs = [h], [tile(d)]
    if attn is not None:
        o, w_o = attn
        args += [o, w_o]
        specs += [tile(o.shape[1]), _resident(w_o.shape)]
    args += [norm_g, w_gu, w_down]
    specs += [_resident((1, d)), _resident(w_gu.shape), _resident(w_down.shape)]
    if final_g is not None:
        args.append(final_g)
        specs.append(_resident((1, d)))
    return pl.pallas_call(
        functools.partial(_ffn_kernel, d_ff=d_ff, ff_chunk=ff_chunk,
                          has_attn=attn is not None, has_final=final_g is not None),
        grid=(n // tm,),
        in_specs=specs,
        out_specs=tile(d),
        out_shape=jax.ShapeDtypeStruct((n, d), F32),
        compiler_params=_compiler_params(("arbitrary",)),
        name="ffn_attn" if attn is not None else "ffn",
    )(*args)


def _qkv_kernel(h_ref, kvg_ref, bg_ref, wkv_ref, wvt_ref, wqt_ref,
                k_ref, v_ref, kb_ref, vt_ref, qt_ref, *, kb):
    h = h_ref[0]
    hn = h * lax.rsqrt(jnp.mean(h * h, axis=-1, keepdims=True) + EPS)
    xkv = (hn * kvg_ref[...]).astype(BF16)
    xq = (hn * bg_ref[...]).astype(BF16)
    kv = _mm(xkv, wkv_ref[...])
    k = kv[:, :KV_DIM]
    k_ref[0] = k
    v_ref[0] = kv[:, KV_DIM:]
    kb_ref[0] = k.astype(BF16)
    vt = _mm_nt(wvt_ref[...], xkv).astype(BF16)
    for g in range(N_KV_HEADS):
        for j in range(vt.shape[1] // kb):
            vt_ref[0, g, j] = vt[g * HEAD_DIM:(g + 1) * HEAD_DIM, j * kb:(j + 1) * kb]
    qt_ref[0] = (_mm_nt(wqt_ref[...], xq) * (HEAD_DIM ** -0.5 * LOG2E)).astype(BF16)


def _qkv(h, kv_norm_g, b_norm_g, w_kv, w_vt, w_qt, kb):
    b, t, d = h.shape
    tm = min(TOKEN_TILE, t)
    assert t % tm == 0 and tm % kb == 0
    hd = w_qt.shape[0]
    return pl.pallas_call(
        functools.partial(_qkv_kernel, kb=kb),
        grid=(b, t // tm),
        in_specs=[
            pl.BlockSpec((1, tm, d), lambda i, j: (i, j, 0)),
            _resident((1, d)), _resident((1, d)),
            _resident(w_kv.shape), _resident(w_vt.shape), _resident(w_qt.shape),
        ],
        out_specs=[
            pl.BlockSpec((1, tm, KV_DIM), lambda i, j: (i, j, 0)),
            pl.BlockSpec((1, tm, KV_DIM), lambda i, j: (i, j, 0)),
            pl.BlockSpec((1, tm, KV_DIM), lambda i, j: (i, j, 0)),
            pl.BlockSpec((1, N_KV_HEADS, tm // kb, HEAD_DIM, kb), lambda i, j: (i, 0, j, 0, 0)),
            pl.BlockSpec((1, hd, tm), lambda i, j: (i, 0, j)),
        ],
        out_shape=[
            jax.ShapeDtypeStruct((b, t, KV_DIM), F32),
            jax.ShapeDtypeStruct((b, t, KV_DIM), F32),
            jax.ShapeDtypeStruct((b, t, KV_DIM), BF16),
            jax.ShapeDtypeStruct((b, N_KV_HEADS, t // kb, HEAD_DIM, kb), BF16),
            jax.ShapeDtypeStruct((b, hd, t), BF16),
        ],
        compiler_params=_compiler_params(("arbitrary", "arbitrary")),
        name="qkv",
    )(h, kv_norm_g, b_norm_g, w_kv, w_vt, w_qt)


ATTN_DOT_BATCH = 8
ATTN_TRI_LAG = 1
ATTN_PV_LAG = 2
EXP2_CLAMP = 126.0
MASKED_LOGIT = -1e30
ZERO_WEIGHT_LOG2 = -160.0


def _softplus2(z):
    return jnp.maximum(z, jnp.log(1.0 + jnp.exp2(jnp.minimum(z, EXP2_CLAMP))) * LOG2E)


def _attn_kernel(q_ref, ki_ref, vi_ref, kd_ref, vd_ref, o_ref,
                 qz_ref, tri_ref, carry_ref, acc_ref,
                 *, lane_groups, tq, n_int_static, transpose_out):
    n_int = pl.program_id(1) if n_int_static is None else n_int_static
    groups = range(len(lane_groups))

    qz_ref[...] = jnp.zeros_like(qz_ref)
    for i, (g, qb) in enumerate(lane_groups):
        qz_ref[i, g * HEAD_DIM:(g + 1) * HEAD_DIM, :] = q_ref[0, qb * HEAD_DIM:(qb + 1) * HEAD_DIM, :]

    si = lax.broadcasted_iota(jnp.int32, (KEY_BLOCK, KEY_BLOCK), 0)
    sj = lax.broadcasted_iota(jnp.int32, (KEY_BLOCK, KEY_BLOCK), 1)
    tri_ref[...] = jnp.where(sj > si, -1.0, 0.0).astype(BF16)

    def key_block(kb, vb, mask, first):
        n = len(lane_groups)
        per = min(ATTN_DOT_BATCH, n)
        assert n % per == 0
        batches = [range(p, p + per) for p in range(0, n, per)]
        zs, ds, spbs, afters = {}, {}, {}, {}
        for step in range(len(batches) + ATTN_PV_LAG):
            if step < len(batches):
                for i in batches[step]:
                    zs[i] = _mm(kb, qz_ref[i])
            if 0 <= step - ATTN_TRI_LAG < len(batches):
                for i in batches[step - ATTN_TRI_LAG]:
                    z = zs.pop(i)
                    sp = _softplus2(z)
                    if mask is None:
                        ds[i] = z - sp
                    else:
                        sp = jnp.where(mask, sp, 0.0)
                        ds[i] = jnp.where(mask, z - sp, MASKED_LOGIT)
                    spbs[i] = sp.astype(BF16)
                for i in batches[step - ATTN_TRI_LAG]:
                    afters[i] = _mm(tri_ref[...], spbs[i])
            if 0 <= step - ATTN_PV_LAG < len(batches):
                atts = {}
                for i in batches[step - ATTN_PV_LAG]:
                    if not first:
                        afters[i] = afters[i] + carry_ref[i, 0:1, :]
                    atts[i] = jnp.exp2(ds.pop(i) + afters[i]).astype(BF16)
                for i in batches[step - ATTN_PV_LAG]:
                    rows = slice(i * HEAD_DIM, (i + 1) * HEAD_DIM)
                    pv = _mm(vb(lane_groups[i][0]), atts[i])
                    acc_ref[rows, :] = pv if first else acc_ref[rows, :] + pv
                    carry_ref[i, 0:1, :] = afters.pop(i)[0:1, :] - spbs.pop(i)[0:1, :].astype(F32)

    mask = (lax.broadcasted_iota(jnp.int32, (KEY_BLOCK, KEY_BLOCK), 0)
            < lax.broadcasted_iota(jnp.int32, (KEY_BLOCK, KEY_BLOCK), 1) % tq)
    key_block(kd_ref[0], lambda g: vd_ref[0, g, 0], mask, True)

    def any_weight_left():
        worst = carry_ref[0, 0:1, :]
        for i in groups[1:]:
            worst = jnp.maximum(worst, carry_ref[i, 0:1, :])
        return jnp.max(worst) > ZERO_WEIGHT_LOG2

    def block(state):
        i, _ = state
        j = n_int - 1 - i
        kb = ki_ref[0, pl.ds(pl.multiple_of(j * KEY_BLOCK, KEY_BLOCK), KEY_BLOCK), :]
        key_block(kb, lambda g: vi_ref[0, g, j], None, False)
        return i + 1, any_weight_left()

    lax.while_loop(lambda state: jnp.logical_and(state[0] < n_int, state[1]), block,
                   (jnp.int32(0), any_weight_left()))

    if transpose_out:
        o_ref[0] = acc_ref[...].T.astype(o_ref.dtype)
    else:
        o_ref[0] = acc_ref[...].astype(o_ref.dtype)


def _attention(q, k_int, vt_int, k_diag, vt_diag, *, lane_groups, tq, n_int_static, transpose_out):
    b, q_rows, lanes_all = q.shape
    nq = lanes_all // KEY_BLOCK
    s_int = k_int.shape[1]
    n_blk = vt_int.shape[2]
    width = len(lane_groups) * HEAD_DIM
    assert s_int == n_blk * KEY_BLOCK and k_diag.shape[1] == nq * KEY_BLOCK
    assert vt_diag.shape[2:] == (nq, HEAD_DIM, KEY_BLOCK)
    if transpose_out:
        out_spec = pl.BlockSpec((1, KEY_BLOCK, width), lambda i, t: (i, t, 0))
        out_shape = jax.ShapeDtypeStruct((b, nq * KEY_BLOCK, width), BF16)
    else:
        out_spec = pl.BlockSpec((1, width, KEY_BLOCK), lambda i, t: (i, 0, t))
        out_shape = jax.ShapeDtypeStruct((b, width, nq * KEY_BLOCK), BF16)
    return pl.pallas_call(
        functools.partial(_attn_kernel, lane_groups=lane_groups, tq=tq,
                          n_int_static=n_int_static, transpose_out=transpose_out),
        grid=(b, nq),
        in_specs=[
            pl.BlockSpec((1, q_rows, KEY_BLOCK), lambda i, t: (i, 0, t)),
            pl.BlockSpec((1, s_int, KV_DIM), lambda i, t: (i, 0, 0)),
            pl.BlockSpec((1, N_KV_HEADS, n_blk, HEAD_DIM, KEY_BLOCK), lambda i, t: (i, 0, 0, 0, 0)),
            pl.BlockSpec((1, KEY_BLOCK, KV_DIM), lambda i, t: (i, t, 0)),
            pl.BlockSpec((1, N_KV_HEADS, 1, HEAD_DIM, KEY_BLOCK), lambda i, t: (i, 0, t, 0, 0)),
        ],
        out_specs=out_spec,
        out_shape=out_shape,
        scratch_shapes=[
            pltpu.VMEM((len(lane_groups), KV_DIM, KEY_BLOCK), BF16),
            pltpu.VMEM((KEY_BLOCK, KEY_BLOCK), BF16),
            pltpu.VMEM((len(lane_groups), V7X_SUBLANES, KEY_BLOCK), F32),
            pltpu.VMEM((width, KEY_BLOCK), F32),
        ],
        compiler_params=_compiler_params(("arbitrary", "arbitrary")),
        name="sb_attention",
    )(q, k_int, vt_int, k_diag, vt_diag)


def _trunk(x, conv_prefix, past, w):
    b, t, d = x.shape
    h, conv_state = _layer0(x, conv_prefix, w)
    tq = min(KEY_BLOCK, t)
    k, v, kb, vt, qt = _qkv(h.reshape(b, t, d), w["kv_norm_g"], w["b_norm_g"],
                            w["w_kv"], w["w_vt"], w["w_qt"], tq)
    if past is None:
        assert tq == KEY_BLOCK
        heads = tuple((hh // GROUP, hh) for hh in range(N_HEADS))
        o = _attention(qt, kb, vt, kb, vt, lane_groups=heads, tq=tq,
                       n_int_static=None, transpose_out=True)
        o = o.reshape(b * t, -1)
    else:
        assert t * GROUP == KEY_BLOCK
        k_past, vt_past = past
        q = (qt.reshape(b, N_KV_HEADS, GROUP, HEAD_DIM, t).transpose(0, 1, 3, 2, 4)
             .reshape(b, KV_DIM, KEY_BLOCK))
        pad = KEY_BLOCK - t
        k_new = jnp.pad(kb, ((0, 0), (0, pad), (0, 0)))
        vt_new = jnp.pad(vt, ((0, 0), (0, 0), (0, 0), (0, 0), (0, pad)))
        o = _attention(q, k_past, vt_past, k_new, vt_new,
                       lane_groups=tuple((g, g) for g in range(N_KV_HEADS)), tq=t,
                       n_int_static=k_past.shape[1] // KEY_BLOCK, transpose_out=False)
        o = (o.reshape(b, N_KV_HEADS, HEAD_DIM, GROUP, t).transpose(0, 4, 1, 3, 2)
             .reshape(b * t, -1))
    y = _ffn(h, w["ffn_norm_g"][1], w["ffn_w_gu"][1], w["ffn_w_down"][1],
             attn=(o, w["w_o"]), final_g=w["final_norm_g"])
    shape4 = (b, t, N_KV_HEADS, HEAD_DIM)
    return y.reshape(b, t, d), conv_state[None], k.reshape(shape4), v.reshape(shape4)


def kernel(x_prompt, x_sample, state_conv, cache_k, cache_v, a_norm_g, conv_w_in, conv_b_in, conv_w_dw, conv_b_dw, conv_ln_g, conv_ln_b, conv_w_out, conv_b_out, kv_norm_g, w_kv, b_norm_g, w_q, w_o, ffn_norm_g, ffn_w_gu, ffn_w_down, final_norm_g):
    assert a_norm_g.shape[0] == 1 and b_norm_g.shape[0] == 1 and ffn_norm_g.shape[0] == 2
    d = x_prompt.shape[-1]
    row = lambda a: a.reshape(1, -1)
    w = {
        "a_norm_g": row(a_norm_g[0]),
        "conv_w_in": conv_w_in[0].astype(BF16), "conv_b_in": row(conv_b_in[0]),
        "conv_w_dw": conv_w_dw[0], "conv_b_dw": row(conv_b_dw[0]),
        "conv_ln_g": row(conv_ln_g[0]), "conv_ln_b": row(conv_ln_b[0]),
        "conv_w_out": conv_w_out[0].astype(BF16), "conv_b_out": row(conv_b_out[0]),
        "kv_norm_g": row(kv_norm_g), "b_norm_g": row(b_norm_g[0]),
        "w_kv": w_kv.astype(BF16),
        "w_vt": w_kv[:, KV_DIM:].T.astype(BF16),
        "w_qt": w_q[0].T.astype(BF16),
        "w_o": w_o[0].astype(BF16),
        "ffn_norm_g": [row(ffn_norm_g[0]), row(ffn_norm_g[1])],
        "ffn_w_gu": [ffn_w_gu[0].astype(BF16), ffn_w_gu[1].astype(BF16)],
        "ffn_w_down": [ffn_w_down[0].astype(BF16), ffn_w_down[1].astype(BF16)],
        "final_norm_g": row(final_norm_g),
    }

    bp = x_prompt.shape[0]
    zero_prefix = jnp.zeros((bp, CONV_STATE, d), x_prompt.dtype)
    y_p, cs_p, k_p, v_p = _trunk(x_prompt, zero_prefix, None, w)

    bs, past_len = cache_k.shape[0], cache_k.shape[1]
    assert past_len % KEY_BLOCK == 0
    k_past = cache_k.reshape(bs, past_len, KV_DIM).astype(BF16)
    vt_past = (cache_v.reshape(bs, past_len // KEY_BLOCK, KEY_BLOCK, N_KV_HEADS, HEAD_DIM)
               .transpose(0, 3, 1, 4, 2).astype(BF16))
    y_s, cs_s, k_s, v_s = _trunk(x_sample, state_conv[0], (k_past, vt_past), w)
    return (y_p, y_s, cs_p, k_p, v_p, cs_s, k_s, v_s)
```

```python
import functools
import math

import jax
import jax.numpy as jnp
from jax import lax
from jax.experimental import pallas as pl
from jax.experimental.pallas import tpu as pltpu

F32 = jnp.float32
BF16 = jnp.bfloat16

EPS = 1e-6
CONV_WIDTH = 31
CONV_STATE = CONV_WIDTH - 1
N_HEADS = 16
N_KV_HEADS = 4
GROUP = N_HEADS // N_KV_HEADS
HEAD_DIM = 64
KV_DIM = N_KV_HEADS * HEAD_DIM
LOG2E = math.log2(math.e)

V7X_SUBLANES = 8
V7X_LANES = 128
V7X_MXU_DIM = 256
V7X_VMEM_LIMIT_BYTES = 56 * 1024 * 1024

TOKEN_TILE = 512
KEY_BLOCK = V7X_MXU_DIM
CONV_HALO = 32
CONV_ROWS = 64
CONV_LANES = 128
GLU_LANES = 512
FFN_LANES = 256


def _compiler_params(semantics):
    return pltpu.CompilerParams(dimension_semantics=semantics,
                                vmem_limit_bytes=V7X_VMEM_LIMIT_BYTES)


def _resident(shape):
    return pl.BlockSpec(shape, lambda *_: (0,) * len(shape), pipeline_mode=pl.Buffered(1))


def _resident_layer(shape, layer):
    return pl.BlockSpec((None,) + tuple(shape[1:]), lambda *_: (layer, 0, 0),
                        pipeline_mode=pl.Buffered(1))


def _rms(x, g):
    return x * lax.rsqrt(jnp.mean(x * x, axis=-1, keepdims=True) + EPS) * g


def _mm(a, b):
    return jnp.dot(a, b, preferred_element_type=F32)


def _mm_nt(a, b):
    return lax.dot_general(a, b, (((1,), (1,)), ((), ())), preferred_element_type=F32)


def _swiglu(h, g_ref, wgu_ref, wd_ref, d_ff, ff_chunk):
    xn = _rms(h, g_ref[...]).astype(BF16)
    acc = h
    for lo in range(0, d_ff, ff_chunk):
        gate = _mm(xn, wgu_ref[:, lo:lo + ff_chunk])
        up = _mm(xn, wgu_ref[:, d_ff + lo:d_ff + lo + ff_chunk])
        mid = (gate * jax.nn.sigmoid(gate) * up).astype(BF16)
        acc = acc + _mm(mid, wd_ref[lo:lo + ff_chunk, :])
    return acc


def _layer0_kernel(x_ref, pre_ref, ag_ref, win_ref, bin_ref, wdw_ref, bdw_ref,
                   lng_ref, lnb_ref, wout_ref, bout_ref, fg_ref, wgu_ref, wd_ref,
                   h2_ref, st_ref, e_ref, y_ref, h1_ref, xf_ref, mid_ref, *, tt, d, nt, d_ff):
    s = pl.program_id(0)
    t = lax.rem(jnp.minimum(s, pl.num_programs(0) - 2), nt)
    sections = iter(pl.when(s >= -i) for i in range(3))

    @pl.when(s == 0)
    def _():
        h1_ref[...] = jnp.zeros_like(h1_ref)

    first_off = CONV_HALO - CONV_STATE
    last_off = first_off + CONV_WIDTH - 1
    out_rows = min(CONV_ROWS, tt)
    conv_chunks = [(r0, c) for r0 in range(0, tt, out_rows) for c in range(d // CONV_LANES)]

    def never(tile):
        return jnp.where(s < 0, tile, 0.0)

    def after(value, dep):
        if dep is None:
            return value
        rows = dep.shape[0]
        return jnp.concatenate([value[0:rows] + dep, value[rows:]], axis=0)

    def conv(chunks, dep=None):
        for r0, c in chunks:
            lanes = slice(c * CONV_LANES, (c + 1) * CONV_LANES)
            acc = after(jnp.broadcast_to(bdw_ref[:, lanes], (out_rows, CONV_LANES)), dep)
            for r in range(V7X_SUBLANES):
                rows = out_rows + (V7X_SUBLANES if r else 0)
                part = None
                for o in range(r, last_off + 1, V7X_SUBLANES):
                    if o < first_off:
                        continue
                    k = o - first_off
                    lo = r0 + o - r
                    wk = jnp.concatenate([wdw_ref[k * V7X_SUBLANES:(k + 1) * V7X_SUBLANES, lanes]]
                                         * (rows // V7X_SUBLANES), axis=0)
                    term = wk * e_ref[lo:lo + rows, lanes]
                    part = term if part is None else part + term
                acc = acc + (pltpu.roll(part, rows - r, axis=0)[0:out_rows] if r else part)
            y_ref[r0:r0 + out_rows, lanes] = acc
            dep = never(acc[0:V7X_SUBLANES])
        return dep

    @next(sections)
    def _():
        xn = _rms(x_ref[0], ag_ref[...]).astype(BF16)
        e_ref[0:CONV_HALO, :] = jnp.where(t == 0, pre_ref[0], e_ref[tt:tt + CONV_HALO, :])
        for lo in range(0, d, GLU_LANES):
            a = _mm(xn, win_ref[:, lo:lo + GLU_LANES]) + bin_ref[:, lo:lo + GLU_LANES]
            g = (_mm(xn, win_ref[:, d + lo:d + lo + GLU_LANES])
                 + bin_ref[:, d + lo:d + lo + GLU_LANES])
            e_ref[CONV_HALO:CONV_HALO + tt, lo:lo + GLU_LANES] = a * jax.nn.sigmoid(g)
        xf_ref[...] = _rms(h1_ref[...], fg_ref[...]).astype(BF16)

    @next(sections)
    def _():
        n_dots = 2 * d_ff // FFN_LANES
        share = lambda i: conv_chunks[len(conv_chunks) * i // n_dots:len(conv_chunks) * (i + 1) // n_dots]
        dot_dep = conv_dep = None
        for i, lo in enumerate(range(0, d_ff, FFN_LANES)):
            conv_dep = conv(share(2 * i), conv_dep)
            gate = after(_mm(xf_ref[...], wgu_ref[:, lo:lo + FFN_LANES]), dot_dep)
            conv_dep = conv(share(2 * i + 1), conv_dep)
            up = after(_mm(xf_ref[...], wgu_ref[:, d_ff + lo:d_ff + lo + FFN_LANES]),
                       never(gate[0:V7X_SUBLANES]))
            mid_ref[:, lo:lo + FFN_LANES] = (gate * jax.nn.sigmoid(gate) * up).astype(BF16)
            dot_dep = never(up[0:V7X_SUBLANES])

    @next(sections)
    def _():
        h2_ref[...] = h1_ref[...] + _mm(mid_ref[...], wd_ref[...])
        y = y_ref[...]
        mu = jnp.mean(y, axis=-1, keepdims=True)
        yc = y - mu
        yn = yc * lax.rsqrt(jnp.mean(yc * yc, axis=-1, keepdims=True) + EPS)
        yn = yn * lng_ref[...] + lnb_ref[...]
        act = (yn * jax.nn.sigmoid(yn)).astype(BF16)
        h1_ref[...] = x_ref[0] + _mm(act, wout_ref[...]) + bout_ref[...]
        st_ref[0] = e_ref[CONV_HALO + tt - CONV_STATE:CONV_HALO + tt, :]


def _layer0(x, prefix, w):
    b, t, d = x.shape
    tt = min(TOKEN_TILE, t)
    nt = t // tt
    n_tiles = b * nt
    w_gu, w_down = w["ffn_w_gu"], w["ffn_w_down"]
    d_ff = w_down.shape[1]
    assert t % tt == 0 and tt % min(CONV_ROWS, tt) == 0 and tt >= CONV_HALO
    assert d % CONV_LANES == 0 and d % GLU_LANES == 0 and d_ff % FFN_LANES == 0
    halo = jnp.pad(prefix, ((0, 0), (CONV_HALO - CONV_STATE, 0), (0, 0)))
    w_dw8 = jnp.repeat(w["conv_w_dw"], V7X_SUBLANES, axis=0)
    mixer_tile = lambda s: jnp.minimum(s, n_tiles - 1)
    row = lambda n: _resident((1, n))
    return pl.pallas_call(
        functools.partial(_layer0_kernel, tt=tt, d=d, nt=nt, d_ff=d_ff),
        grid=(n_tiles + 1,),
        in_specs=[
            pl.BlockSpec((1, tt, d), lambda s: (mixer_tile(s) // nt, mixer_tile(s) % nt, 0)),
            pl.BlockSpec((1, CONV_HALO, d), lambda s: (mixer_tile(s) // nt, 0, 0)),
            row(d), _resident((d, 2 * d)), row(2 * d), _resident(w_dw8.shape), row(d),
            row(d), row(d), _resident((d, d)), row(d),
            row(d), _resident_layer(w_gu.shape, 0), _resident_layer(w_down.shape, 0),
        ],
        out_specs=[
            pl.BlockSpec((tt, d), lambda s: (jnp.maximum(s - 1, 0), 0)),
            pl.BlockSpec((1, CONV_STATE, d), lambda s: (mixer_tile(s) // nt, 0, 0)),
        ],
        out_shape=[jax.ShapeDtypeStruct((b * t, d), F32),
                   jax.ShapeDtypeStruct((b, CONV_STATE, d), F32)],
        scratch_shapes=[
            pltpu.VMEM((CONV_HALO + tt, d), F32),
            pltpu.VMEM((tt, d), F32),
            pltpu.VMEM((tt, d), F32),
            pltpu.VMEM((tt, d), BF16),
            pltpu.VMEM((tt, d_ff), BF16),
        ],
        compiler_params=_compiler_params(("arbitrary",)),
        name="layer0",
    )(x, halo, w["a_norm_g"], w["conv_w_in"], w["conv_b_in"], w_dw8, w["conv_b_dw"],
      w["conv_ln_g"], w["conv_ln_b"], w["conv_w_out"], w["conv_b_out"],
      w["ffn_norm_g"][0], w_gu, w_down)


def _ffn_kernel(*refs, d_ff, ff_chunk, has_attn, has_final):
    refs = list(refs)
    h_ref = refs.pop(0)
    if has_attn:
        o_ref, wo_ref = refs.pop(0), refs.pop(0)
    g_ref, wgu_ref, wd_ref = refs.pop(0), refs.pop(0), refs.pop(0)
    if has_final:
        fg_ref = refs.pop(0)
    out_ref, = refs

    h = h_ref[...]
    if has_attn:
        h = h + _mm(o_ref[...], wo_ref[...])
    acc = _swiglu(h, g_ref, wgu_ref, wd_ref, d_ff, ff_chunk)
    if has_final:
        acc = _rms(acc, fg_ref[...])
    out_ref[...] = acc


def _ffn(h, norm_g, w_gu, w_down, layer, attn=None, final_g=None):
    n, d = h.shape
    d_ff = w_down.shape[1]
    tm = min(TOKEN_TILE, n)
    ff_chunk = d_ff // 2 if (d_ff // 2) % V7X_LANES == 0 else d_ff
    assert n % tm == 0 and d_ff % ff_chunk == 0
    tile = lambda width: pl.BlockSpec((tm, width), lambda i: (i, 0))
    args, specs = [h], [tile(d)]
    if attn is not None:
        o, w_o = attn
        args += [o, w_o]
        specs += [tile(o.shape[1]), _resident(w_o.shape)]
    args += [norm_g, w_gu, w_down]
    specs += [_resident((1, d)), _resident_layer(w_gu.shape, layer),
              _resident_layer(w_down.shape, layer)]
    if final_g is not None:
        args.append(final_g)
        specs.append(_resident((1, d)))
    return pl.pallas_call(
        functools.partial(_ffn_kernel, d_ff=d_ff, ff_chunk=ff_chunk,
                          has_attn=attn is not None, has_final=final_g is not None),
        grid=(n // tm,),
        in_specs=specs,
        out_specs=tile(d),
        out_shape=jax.ShapeDtypeStruct((n, d), F32),
        compiler_params=_compiler_params(("arbitrary",)),
        name="ffn_attn" if attn is not None else "ffn",
    )(*args)


def _qkv_kernel(h_ref, kvg_ref, bg_ref, wkv_ref, wvt_ref, wqt_ref,
                k_ref, v_ref, kb_ref, vt_ref, qt_ref, *, kb):
    h = h_ref[0]
    hn = h * lax.rsqrt(jnp.mean(h * h, axis=-1, keepdims=True) + EPS)
    xkv = (hn * kvg_ref[...]).astype(BF16)
    xq = (hn * bg_ref[...]).astype(BF16)
    kv = _mm(xkv, wkv_ref[...])
    k = kv[:, :KV_DIM]
    k_ref[0] = k
    v_ref[0] = kv[:, KV_DIM:]
    kb_ref[0] = k.astype(BF16)
    vt = _mm_nt(wvt_ref[...], xkv).astype(BF16)
    for g in range(N_KV_HEADS):
        for j in range(vt.shape[1] // kb):
            vt_ref[0, g, j] = vt[g * HEAD_DIM:(g + 1) * HEAD_DIM, j * kb:(j + 1) * kb]
    qt_ref[0] = (_mm_nt(wqt_ref[...], xq) * (HEAD_DIM ** -0.5 * LOG2E)).astype(BF16)


def _qkv(h, kv_norm_g, b_norm_g, w_kv, w_vt, w_qt, kb):
    b, t, d = h.shape
    tm = min(TOKEN_TILE, t)
    assert t % tm == 0 and tm % kb == 0
    hd = w_qt.shape[0]
    return pl.pallas_call(
        functools.partial(_qkv_kernel, kb=kb),
        grid=(b, t // tm),
        in_specs=[
            pl.BlockSpec((1, tm, d), lambda i, j: (i, j, 0)),
            _resident((1, d)), _resident((1, d)),
            _resident(w_kv.shape), _resident(w_vt.shape), _resident(w_qt.shape),
        ],
        out_specs=[
            pl.BlockSpec((1, tm, KV_DIM), lambda i, j: (i, j, 0)),
            pl.BlockSpec((1, tm, KV_DIM), lambda i, j: (i, j, 0)),
            pl.BlockSpec((1, tm, KV_DIM), lambda i, j: (i, j, 0)),
            pl.BlockSpec((1, N_KV_HEADS, tm // kb, HEAD_DIM, kb), lambda i, j: (i, 0, j, 0, 0)),
            pl.BlockSpec((1, hd, tm), lambda i, j: (i, 0, j)),
        ],
        out_shape=[
            jax.ShapeDtypeStruct((b, t, KV_DIM), F32),
            jax.ShapeDtypeStruct((b, t, KV_DIM), F32),
            jax.ShapeDtypeStruct((b, t, KV_DIM), BF16),
            jax.ShapeDtypeStruct((b, N_KV_HEADS, t // kb, HEAD_DIM, kb), BF16),
            jax.ShapeDtypeStruct((b, hd, t), BF16),
        ],
        compiler_params=_compiler_params(("arbitrary", "arbitrary")),
        name="qkv",
    )(h, kv_norm_g, b_norm_g, w_kv, w_vt, w_qt)


ATTN_DOT_BATCH = 8
ATTN_TRI_LAG = 1
ATTN_PV_LAG = 2
EXP2_CLAMP = 126.0
MASKED_LOGIT = -1e30
ZERO_WEIGHT_LOG2 = -160.0


def _softplus2(z):
    return jnp.maximum(z, jnp.log(1.0 + jnp.exp2(jnp.minimum(z, EXP2_CLAMP))) * LOG2E)


def _attn_kernel(q_ref, ki_ref, vi_ref, kd_ref, vd_ref, o_ref,
                 qz_ref, tri_ref, carry_ref, acc_ref,
                 *, lane_groups, tq, n_int_static, transpose_out):
    n_int = pl.program_id(1) if n_int_static is None else n_int_static
    groups = range(len(lane_groups))

    qz_ref[...] = jnp.zeros_like(qz_ref)
    for i, (g, qb) in enumerate(lane_groups):
        qz_ref[i, g * HEAD_DIM:(g + 1) * HEAD_DIM, :] = q_ref[0, qb * HEAD_DIM:(qb + 1) * HEAD_DIM, :]

    si = lax.broadcasted_iota(jnp.int32, (KEY_BLOCK, KEY_BLOCK), 0)
    sj = lax.broadcasted_iota(jnp.int32, (KEY_BLOCK, KEY_BLOCK), 1)
    tri_ref[...] = jnp.where(sj > si, -1.0, 0.0).astype(BF16)

    def key_blocks(blocks):
        items = [(blk, i) for blk in range(len(blocks)) for i in groups]
        per = min(ATTN_DOT_BATCH, len(lane_groups))
        assert len(lane_groups) % per == 0
        batches = [items[p:p + per] for p in range(0, len(items), per)]
        zs, ds, spbs, afters = {}, {}, {}, {}
        for step in range(len(batches) + ATTN_PV_LAG):
            if step < len(batches):
                for it in batches[step]:
                    zs[it] = _mm(blocks[it[0]][0], qz_ref[it[1]])
            if 0 <= step - ATTN_TRI_LAG < len(batches):
                for it in batches[step - ATTN_TRI_LAG]:
                    mask = blocks[it[0]][2]
                    z = zs.pop(it)
                    sp = _softplus2(z)
                    if mask is None:
                        ds[it] = z - sp
                    else:
                        sp = jnp.where(mask, sp, 0.0)
                        ds[it] = jnp.where(mask, z - sp, MASKED_LOGIT)
                    spbs[it] = sp.astype(BF16)
                for it in batches[step - ATTN_TRI_LAG]:
                    afters[it] = _mm(tri_ref[...], spbs[it])
            if 0 <= step - ATTN_PV_LAG < len(batches):
                atts = {}
                for it in batches[step - ATTN_PV_LAG]:
                    if not blocks[it[0]][3]:
                        afters[it] = afters[it] + carry_ref[it[1], 0:1, :]
                    atts[it] = jnp.exp2(ds.pop(it) + afters[it]).astype(BF16)
                    carry_ref[it[1], 0:1, :] = (afters.pop(it)[0:1, :]
                                                - spbs.pop(it)[0:1, :].astype(F32))
                for it in batches[step - ATTN_PV_LAG]:
                    _, vb, _, first = blocks[it[0]]
                    rows = slice(it[1] * HEAD_DIM, (it[1] + 1) * HEAD_DIM)
                    pv = _mm(vb(lane_groups[it[1]][0]), atts[it])
                    acc_ref[rows, :] = pv if first else acc_ref[rows, :] + pv

    mask = (lax.broadcasted_iota(jnp.int32, (KEY_BLOCK, KEY_BLOCK), 0)
            < lax.broadcasted_iota(jnp.int32, (KEY_BLOCK, KEY_BLOCK), 1) % tq)
    diag = (kd_ref[0], lambda g: vd_ref[0, g, 0], mask, True)

    def interior(j):
        kb = ki_ref[0, pl.ds(pl.multiple_of(j * KEY_BLOCK, KEY_BLOCK), KEY_BLOCK), :]
        return (kb, lambda g: vi_ref[0, g, j], None, False)

    if n_int_static is None:
        pl.when(n_int == 0)(lambda: key_blocks([diag]))
        pl.when(n_int > 0)(lambda: key_blocks([diag, interior(n_int - 1)]))
    elif n_int_static == 0:
        key_blocks([diag])
    else:
        key_blocks([diag, interior(n_int - 1)])

    def any_weight_left():
        worst = carry_ref[0, 0:1, :]
        for i in groups[1:]:
            worst = jnp.maximum(worst, carry_ref[i, 0:1, :])
        return jnp.max(worst) > ZERO_WEIGHT_LOG2

    def block(state):
        i, _ = state
        key_blocks([interior(n_int - 1 - i)])
        return i + 1, any_weight_left()

    lax.while_loop(lambda state: jnp.logical_and(state[0] < n_int, state[1]), block,
                   (jnp.int32(1), any_weight_left()))

    if transpose_out:
        o_ref[0] = acc_ref[...].T.astype(o_ref.dtype)
    else:
        o_ref[0] = acc_ref[...].astype(o_ref.dtype)


def _attention(q, k_int, vt_int, k_diag, vt_diag, *, lane_groups, tq, n_int_static, transpose_out):
    b, q_rows, lanes_all = q.shape
    nq = lanes_all // KEY_BLOCK
    s_int = k_int.shape[1]
    n_blk = vt_int.shape[2]
    width = len(lane_groups) * HEAD_DIM
    assert s_int == n_blk * KEY_BLOCK and k_diag.shape[1] == nq * KEY_BLOCK
    assert vt_diag.shape[2:] == (nq, HEAD_DIM, KEY_BLOCK)
    if transpose_out:
        out_spec = pl.BlockSpec((1, KEY_BLOCK, width), lambda i, t: (i, t, 0))
        out_shape = jax.ShapeDtypeStruct((b, nq * KEY_BLOCK, width), BF16)
    else:
        out_spec = pl.BlockSpec((1, width, KEY_BLOCK), lambda i, t: (i, 0, t))
        out_shape = jax.ShapeDtypeStruct((b, width, nq * KEY_BLOCK), BF16)
    return pl.pallas_call(
        functools.partial(_attn_kernel, lane_groups=lane_groups, tq=tq,
                          n_int_static=n_int_static, transpose_out=transpose_out),
        grid=(b, nq),
        in_specs=[
            pl.BlockSpec((1, q_rows, KEY_BLOCK), lambda i, t: (i, 0, t)),
            pl.BlockSpec((1, s_int, KV_DIM), lambda i, t: (i, 0, 0)),
            pl.BlockSpec((1, N_KV_HEADS, n_blk, HEAD_DIM, KEY_BLOCK), lambda i, t: (i, 0, 0, 0, 0)),
            pl.BlockSpec((1, KEY_BLOCK, KV_DIM), lambda i, t: (i, t, 0)),
            pl.BlockSpec((1, N_KV_HEADS, 1, HEAD_DIM, KEY_BLOCK), lambda i, t: (i, 0, t, 0, 0)),
        ],
        out_specs=out_spec,
        out_shape=out_shape,
        scratch_shapes=[
            pltpu.VMEM((len(lane_groups), KV_DIM, KEY_BLOCK), BF16),
            pltpu.VMEM((KEY_BLOCK, KEY_BLOCK), BF16),
            pltpu.VMEM((len(lane_groups), V7X_SUBLANES, KEY_BLOCK), F32),
            pltpu.VMEM((width, KEY_BLOCK), F32),
        ],
        compiler_params=_compiler_params(("arbitrary", "arbitrary")),
        name="sb_attention",
    )(q, k_int, vt_int, k_diag, vt_diag)


def _trunk(x, conv_prefix, past, w):
    b, t, d = x.shape
    h, conv_state = _layer0(x, conv_prefix, w)
    tq = min(KEY_BLOCK, t)
    k, v, kb, vt, qt = _qkv(h.reshape(b, t, d), w["kv_norm_g"], w["b_norm_g"],
                            w["w_kv"], w["w_vt"], w["w_qt"], tq)
    if past is None:
        assert tq == KEY_BLOCK
        heads = tuple((hh // GROUP, hh) for hh in range(N_HEADS))
        o = _attention(qt, kb, vt, kb, vt, lane_groups=heads, tq=tq,
                       n_int_static=None, transpose_out=True)
        o = o.reshape(b * t, -1)
    else:
        assert t * GROUP == KEY_BLOCK
        k_past, vt_past = past
        q = (qt.reshape(b, N_KV_HEADS, GROUP, HEAD_DIM, t).transpose(0, 1, 3, 2, 4)
             .reshape(b, KV_DIM, KEY_BLOCK))
        pad = KEY_BLOCK - t
        k_new = jnp.pad(kb, ((0, 0), (0, pad), (0, 0)))
        vt_new = jnp.pad(vt, ((0, 0), (0, 0), (0, 0), (0, 0), (0, pad)))
        o = _attention(q, k_past, vt_past, k_new, vt_new,
                       lane_groups=tuple((g, g) for g in range(N_KV_HEADS)), tq=t,
                       n_int_static=k_past.shape[1] // KEY_BLOCK, transpose_out=False)
        o = (o.reshape(b, N_KV_HEADS, HEAD_DIM, GROUP, t).transpose(0, 4, 1, 3, 2)
             .reshape(b * t, -1))
    y = _ffn(h, w["ffn_norm_g"][1], w["ffn_w_gu"], w["ffn_w_down"], 1,
             attn=(o, w["w_o"]), final_g=w["final_norm_g"])
    shape4 = (b, t, N_KV_HEADS, HEAD_DIM)
    return y.reshape(b, t, d), conv_state[None], k.reshape(shape4), v.reshape(shape4)


def kernel(x_prompt, x_sample, state_conv, cache_k, cache_v, a_norm_g, conv_w_in, conv_b_in, conv_w_dw, conv_b_dw, conv_ln_g, conv_ln_b, conv_w_out, conv_b_out, kv_norm_g, w_kv, b_norm_g, w_q, w_o, ffn_norm_g, ffn_w_gu, ffn_w_down, final_norm_g):
    assert a_norm_g.shape[0] == 1 and b_norm_g.shape[0] == 1 and ffn_norm_g.shape[0] == 2
    d = x_prompt.shape[-1]
    row = lambda a: a.reshape(1, -1)
    w = {
        "a_norm_g": row(a_norm_g[0]),
        "conv_w_in": conv_w_in[0].astype(BF16), "conv_b_in": row(conv_b_in[0]),
        "conv_w_dw": conv_w_dw[0], "conv_b_dw": row(conv_b_dw[0]),
        "conv_ln_g": row(conv_ln_g[0]), "conv_ln_b": row(conv_ln_b[0]),
        "conv_w_out": conv_w_out[0].astype(BF16), "conv_b_out": row(conv_b_out[0]),
        "kv_norm_g": row(kv_norm_g), "b_norm_g": row(b_norm_g[0]),
        "w_kv": w_kv.astype(BF16),
        "w_vt": w_kv[:, KV_DIM:].T.astype(BF16),
        "w_qt": w_q[0].T.astype(BF16),
        "w_o": w_o[0].astype(BF16),
        "ffn_norm_g": [row(ffn_norm_g[0]), row(ffn_norm_g[1])],
        "ffn_w_gu": ffn_w_gu.astype(BF16), "ffn_w_down": ffn_w_down.astype(BF16),
        "final_norm_g": row(final_norm_g),
    }

    bp = x_prompt.shape[0]
    zero_prefix = jnp.zeros((bp, CONV_STATE, d), x_prompt.dtype)
    y_p, cs_p, k_p, v_p = _trunk(x_prompt, zero_prefix, None, w)

    bs, past_len = cache_k.shape[0], cache_k.shape[1]
    assert past_len % KEY_BLOCK == 0
    k_past = cache_k.reshape(bs, past_len, KV_DIM).astype(BF16)
    vt_past = (cache_v.reshape(bs, past_len // KEY_BLOCK, KEY_BLOCK, N_KV_HEADS, HEAD_DIM)
               .transpose(0, 3, 1, 4, 2).astype(BF16))
    y_s, cs_s, k_s, v_s = _trunk(x_sample, state_conv[0], (k_past, vt_past), w)
    return (y_p, y_s, cs_p, k_p, v_p, cs_s, k_s, v_s)
```

```python
import functools
import math

import jax
import jax.numpy as jnp
from jax import lax
from jax.experimental import pallas as pl
from jax.experimental.pallas import tpu as pltpu

F32 = jnp.float32
BF16 = jnp.bfloat16

EPS = 1e-6
CONV_WIDTH = 31
CONV_STATE = CONV_WIDTH - 1
N_HEADS = 16
N_KV_HEADS = 4
GROUP = N_HEADS // N_KV_HEADS
HEAD_DIM = 64
KV_DIM = N_KV_HEADS * HEAD_DIM
LOG2E = math.log2(math.e)

V7X_SUBLANES = 8
V7X_LANES = 128
V7X_MXU_DIM = 256
V7X_VMEM_LIMIT_BYTES = 56 * 1024 * 1024

TOKEN_TILE = 512
KEY_BLOCK = V7X_MXU_DIM
CONV_HALO = 32
CONV_ROWS = 64
CONV_LANES = 128
GLU_LANES = 512
FFN_LANES = 256


def _compiler_params(semantics):
    return pltpu.CompilerParams(dimension_semantics=semantics,
                                vmem_limit_bytes=V7X_VMEM_LIMIT_BYTES)


def _resident(shape):
    return pl.BlockSpec(shape, lambda *_: (0,) * len(shape), pipeline_mode=pl.Buffered(1))


def _resident_layer(shape, layer):
    return pl.BlockSpec((None,) + tuple(shape[1:]), lambda *_: (layer, 0, 0),
                        pipeline_mode=pl.Buffered(1))


def _rms(x, g):
    return x * lax.rsqrt(jnp.mean(x * x, axis=-1, keepdims=True) + EPS) * g


def _mm(a, b):
    return jnp.dot(a, b, preferred_element_type=F32)


def _mm_nt(a, b):
    return lax.dot_general(a, b, (((1,), (1,)), ((), ())), preferred_element_type=F32)


def _swish_bf16(x, y=None):
    h = 0.5 * x
    out = h + h * jnp.tanh(h)
    return (out if y is None else out * y).astype(BF16)


def _swiglu(h, g_ref, wgu_ref, wd_ref, d_ff, ff_chunk):
    xn = _rms(h, g_ref[...]).astype(BF16)
    acc = h
    for lo in range(0, d_ff, ff_chunk):
        gate = _mm(xn, wgu_ref[:, lo:lo + ff_chunk])
        up = _mm(xn, wgu_ref[:, d_ff + lo:d_ff + lo + ff_chunk])
        acc = acc + _mm(_swish_bf16(gate, up), wd_ref[lo:lo + ff_chunk, :])
    return acc


def _layer0_kernel(x_ref, pre_ref, ag_ref, win_ref, bin_ref, wdw_ref, bdw_ref,
                   lng_ref, lnb_ref, wout_ref, bout_ref, fg_ref, wgu_ref, wd_ref,
                   h2_ref, st_ref, e_ref, y_ref, h1_ref, xf_ref, mid_ref, *, tt, d, nt, d_ff):
    s = pl.program_id(0)
    t = lax.rem(jnp.minimum(s, pl.num_programs(0) - 2), nt)
    sections = iter(pl.when(s >= -i) for i in range(3))

    @pl.when(s == 0)
    def _():
        h1_ref[...] = jnp.zeros_like(h1_ref)

    first_off = CONV_HALO - CONV_STATE
    last_off = first_off + CONV_WIDTH - 1
    out_rows = min(CONV_ROWS, tt)
    conv_chunks = [(r0, c) for r0 in range(0, tt, out_rows) for c in range(d // CONV_LANES)]

    def conv(chunks):
        for r0, c in chunks:
            lanes = slice(c * CONV_LANES, (c + 1) * CONV_LANES)
            acc = jnp.broadcast_to(bdw_ref[:, lanes], (out_rows, CONV_LANES))
            for r in range(V7X_SUBLANES):
                rows = out_rows + (V7X_SUBLANES if r else 0)
                part = None
                for o in range(r, last_off + 1, V7X_SUBLANES):
                    if o < first_off:
                        continue
                    k = o - first_off
                    lo = r0 + o - r
                    wk = jnp.concatenate([wdw_ref[k * V7X_SUBLANES:(k + 1) * V7X_SUBLANES, lanes]]
                                         * (rows // V7X_SUBLANES), axis=0)
                    term = wk * e_ref[lo:lo + rows, lanes]
                    part = term if part is None else part + term
                acc = acc + (pltpu.roll(part, rows - r, axis=0)[0:out_rows] if r else part)
            y_ref[r0:r0 + out_rows, lanes] = acc

    @next(sections)
    def _():
        xn = _rms(x_ref[0], ag_ref[...]).astype(BF16)
        e_ref[0:CONV_HALO, :] = jnp.where(t == 0, pre_ref[0], e_ref[tt:tt + CONV_HALO, :])
        for lo in range(0, d, GLU_LANES):
            a = _mm(xn, win_ref[:, lo:lo + GLU_LANES]) + bin_ref[:, lo:lo + GLU_LANES]
            g = (_mm(xn, win_ref[:, d + lo:d + lo + GLU_LANES])
                 + bin_ref[:, d + lo:d + lo + GLU_LANES])
            e_ref[CONV_HALO:CONV_HALO + tt, lo:lo + GLU_LANES] = a * (0.5 + 0.5 * jnp.tanh(0.5 * g))
        xf_ref[...] = _rms(h1_ref[...], fg_ref[...]).astype(BF16)

    @next(sections)
    def _():
        for lo in range(0, d_ff, FFN_LANES):
            gate = _mm(xf_ref[...], wgu_ref[:, lo:lo + FFN_LANES])
            up = _mm(xf_ref[...], wgu_ref[:, d_ff + lo:d_ff + lo + FFN_LANES])
            mid_ref[:, lo:lo + FFN_LANES] = _swish_bf16(gate, up)
        conv(conv_chunks)

    @next(sections)
    def _():
        h2_ref[...] = h1_ref[...] + _mm(mid_ref[...], wd_ref[...])
        y = y_ref[...]
        mu = jnp.mean(y, axis=-1, keepdims=True)
        yc = y - mu
        yn = yc * lax.rsqrt(jnp.mean(yc * yc, axis=-1, keepdims=True) + EPS)
        yn = yn * lng_ref[...] + lnb_ref[...]
        h1_ref[...] = x_ref[0] + _mm(_swish_bf16(yn), wout_ref[...]) + bout_ref[...]
        st_ref[0] = e_ref[CONV_HALO + tt - CONV_STATE:CONV_HALO + tt, :]


def _layer0(x, prefix, w):
    b, t, d = x.shape
    tt = min(TOKEN_TILE, t)
    nt = t // tt
    n_tiles = b * nt
    w_gu, w_down = w["ffn_w_gu"], w["ffn_w_down"]
    d_ff = w_down.shape[1]
    assert t % tt == 0 and tt % min(CONV_ROWS, tt) == 0 and tt >= CONV_HALO
    assert d % CONV_LANES == 0 and d % GLU_LANES == 0 and d_ff % FFN_LANES == 0
    halo = jnp.pad(prefix, ((0, 0), (CONV_HALO - CONV_STATE, 0), (0, 0)))
    w_dw8 = jnp.repeat(w["conv_w_dw"], V7X_SUBLANES, axis=0)
    mixer_tile = lambda s: jnp.minimum(s, n_tiles - 1)
    row = lambda n: _resident((1, n))
    return pl.pallas_call(
        functools.partial(_layer0_kernel, tt=tt, d=d, nt=nt, d_ff=d_ff),
        grid=(n_tiles + 1,),
        in_specs=[
            pl.BlockSpec((1, tt, d), lambda s: (mixer_tile(s) // nt, mixer_tile(s) % nt, 0)),
            pl.BlockSpec((1, CONV_HALO, d), lambda s: (mixer_tile(s) // nt, 0, 0)),
            row(d), _resident((d, 2 * d)), row(2 * d), _resident(w_dw8.shape), row(d),
            row(d), row(d), _resident((d, d)), row(d),
            row(d), _resident_layer(w_gu.shape, 0), _resident_layer(w_down.shape, 0),
        ],
        out_specs=[
            pl.BlockSpec((tt, d), lambda s: (jnp.maximum(s - 1, 0), 0)),
            pl.BlockSpec((1, CONV_STATE, d), lambda s: (mixer_tile(s) // nt, 0, 0)),
        ],
        out_shape=[jax.ShapeDtypeStruct((b * t, d), F32),
                   jax.ShapeDtypeStruct((b, CONV_STATE, d), F32)],
        scratch_shapes=[
            pltpu.VMEM((CONV_HALO + tt, d), F32),
            pltpu.VMEM((tt, d), F32),
            pltpu.VMEM((tt, d), F32),
            pltpu.VMEM((tt, d), BF16),
            pltpu.VMEM((tt, d_ff), BF16),
        ],
        compiler_params=_compiler_params(("arbitrary",)),
        name="layer0",
    )(x, halo, w["a_norm_g"], w["conv_w_in"], w["conv_b_in"], w_dw8, w["conv_b_dw"],
      w["conv_ln_g"], w["conv_ln_b"], w["conv_w_out"], w["conv_b_out"],
      w["ffn_norm_g"][0], w_gu, w_down)


def _ffn_kernel(*refs, d_ff, ff_chunk, has_attn, has_final):
    refs = list(refs)
    h_ref = refs.pop(0)
    if has_attn:
        o_ref, wo_ref = refs.pop(0), refs.pop(0)
    g_ref, wgu_ref, wd_ref = refs.pop(0), refs.pop(0), refs.pop(0)
    if has_final:
        fg_ref = refs.pop(0)
    out_ref, = refs

    h = h_ref[...]
    if has_attn:
        h = h + _mm(o_ref[...], wo_ref[...])
    acc = _swiglu(h, g_ref, wgu_ref, wd_ref, d_ff, ff_chunk)
    if has_final:
        acc = _rms(acc, fg_ref[...])
    out_ref[...] = acc


def _ffn(h, norm_g, w_gu, w_down, layer, attn=None, final_g=None):
    n, d = h.shape
    d_ff = w_down.shape[1]
    tm = min(TOKEN_TILE, n)
    ff_chunk = d_ff // 2 if (d_ff // 2) % V7X_LANES == 0 else d_ff
    assert n % tm == 0 and d_ff % ff_chunk == 0
    tile = lambda width: pl.BlockSpec((tm, width), lambda i: (i, 0))
    args, specs = [h], [tile(d)]
    if attn is not None:
        o, w_o = attn
        args += [o, w_o]
        specs += [tile(o.shape[1]), _resident(w_o.shape)]
    args += [norm_g, w_gu, w_down]
    specs += [_resident((1, d)), _resident_layer(w_gu.shape, layer),
              _resident_layer(w_down.shape, layer)]
    if final_g is not None:
        args.append(final_g)
        specs.append(_resident((1, d)))
    return pl.pallas_call(
        functools.partial(_ffn_kernel, d_ff=d_ff, ff_chunk=ff_chunk,
                          has_attn=attn is not None, has_final=final_g is not None),
        grid=(n // tm,),
        in_specs=specs,
        out_specs=tile(d),
        out_shape=jax.ShapeDtypeStruct((n, d), F32),
        compiler_params=_compiler_params(("arbitrary",)),
        name="ffn_attn" if attn is not None else "ffn",
    )(*args)


def _qkv_kernel(h_ref, kvg_ref, bg_ref, wkv_ref, wvt_ref, wqt_ref,
                k_ref, v_ref, kb_ref, vt_ref, qt_ref, *, kb):
    h = h_ref[0]
    hn = h * lax.rsqrt(jnp.mean(h * h, axis=-1, keepdims=True) + EPS)
    xkv = (hn * kvg_ref[...]).astype(BF16)
    xq = (hn * bg_ref[...]).astype(BF16)
    kv = _mm(xkv, wkv_ref[...])
    k = kv[:, :KV_DIM]
    k_ref[0] = k
    v_ref[0] = kv[:, KV_DIM:]
    kb_ref[0] = k.astype(BF16)
    vt = _mm_nt(wvt_ref[...], xkv).astype(BF16)
    for g in range(N_KV_HEADS):
        for j in range(vt.shape[1] // kb):
            vt_ref[0, g, j] = vt[g * HEAD_DIM:(g + 1) * HEAD_DIM, j * kb:(j + 1) * kb]
    qt_ref[0] = (_mm_nt(wqt_ref[...], xq) * (HEAD_DIM ** -0.5 * LOG2E)).astype(BF16)


def _qkv(h, kv_norm_g, b_norm_g, w_kv, w_vt, w_qt, kb):
    b, t, d = h.shape
    tm = min(TOKEN_TILE, t)
    assert t % tm == 0 and tm % kb == 0
    hd = w_qt.shape[0]
    return pl.pallas_call(
        functools.partial(_qkv_kernel, kb=kb),
        grid=(b, t // tm),
        in_specs=[
            pl.BlockSpec((1, tm, d), lambda i, j: (i, j, 0)),
            _resident((1, d)), _resident((1, d)),
            _resident(w_kv.shape), _resident(w_vt.shape), _resident(w_qt.shape),
        ],
        out_specs=[
            pl.BlockSpec((1, tm, KV_DIM), lambda i, j: (i, j, 0)),
            pl.BlockSpec((1, tm, KV_DIM), lambda i, j: (i, j, 0)),
            pl.BlockSpec((1, tm, KV_DIM), lambda i, j: (i, j, 0)),
            pl.BlockSpec((1, N_KV_HEADS, tm // kb, HEAD_DIM, kb), lambda i, j: (i, 0, j, 0, 0)),
            pl.BlockSpec((1, hd, tm), lambda i, j: (i, 0, j)),
        ],
        out_shape=[
            jax.ShapeDtypeStruct((b, t, KV_DIM), F32),
            jax.ShapeDtypeStruct((b, t, KV_DIM), F32),
            jax.ShapeDtypeStruct((b, t, KV_DIM), BF16),
            jax.ShapeDtypeStruct((b, N_KV_HEADS, t // kb, HEAD_DIM, kb), BF16),
            jax.ShapeDtypeStruct((b, hd, t), BF16),
        ],
        compiler_params=_compiler_params(("arbitrary", "arbitrary")),
        name="qkv",
    )(h, kv_norm_g, b_norm_g, w_kv, w_vt, w_qt)


ATTN_DOT_BATCH = 8
ATTN_TRI_LAG = 1
ATTN_PV_LAG = 2
EXP2_CLAMP = 126.0
MASKED_LOGIT = -1e30
ZERO_WEIGHT_LOG2 = -160.0


def _softplus2(z):
    return jnp.maximum(z, jnp.log(1.0 + jnp.exp2(jnp.minimum(z, EXP2_CLAMP))) * LOG2E)


def _attn_kernel(q_ref, ki_ref, vi_ref, kd_ref, vd_ref, o_ref,
                 qz_ref, tri_ref, carry_ref, acc_ref,
                 *, lane_groups, tq, n_int_static, transpose_out):
    n_int = pl.program_id(1) if n_int_static is None else n_int_static
    groups = range(len(lane_groups))

    @pl.when(jnp.logical_and(pl.program_id(0) == 0, pl.program_id(1) == 0))
    def _():
        qz_ref[...] = jnp.zeros_like(qz_ref)
        si = lax.broadcasted_iota(jnp.int32, (KEY_BLOCK, KEY_BLOCK), 0)
        sj = lax.broadcasted_iota(jnp.int32, (KEY_BLOCK, KEY_BLOCK), 1)
        tri_ref[...] = jnp.where(sj > si, -1.0, 0.0).astype(BF16)

    for i, (g, qb) in enumerate(lane_groups):
        qz_ref[i, g * HEAD_DIM:(g + 1) * HEAD_DIM, :] = q_ref[0, qb * HEAD_DIM:(qb + 1) * HEAD_DIM, :]

    def key_blocks(blocks):
        items = [(blk, i) for blk in range(len(blocks)) for i in groups]
        per = min(ATTN_DOT_BATCH, len(lane_groups))
        assert len(lane_groups) % per == 0
        batches = [items[p:p + per] for p in range(0, len(items), per)]
        zs, ds, spbs, afters = {}, {}, {}, {}
        for step in range(len(batches) + ATTN_PV_LAG):
            if step < len(batches):
                for it in batches[step]:
                    zs[it] = _mm(blocks[it[0]][0], qz_ref[it[1]])
            if 0 <= step - ATTN_TRI_LAG < len(batches):
                for it in batches[step - ATTN_TRI_LAG]:
                    mask = blocks[it[0]][2]
                    z = zs.pop(it)
                    sp = _softplus2(z)
                    if mask is None:
                        ds[it] = z - sp
                    else:
                        sp = jnp.where(mask, sp, 0.0)
                        ds[it] = jnp.where(mask, z - sp, MASKED_LOGIT)
                    spbs[it] = sp.astype(BF16)
                for it in batches[step - ATTN_TRI_LAG]:
                    afters[it] = _mm(tri_ref[...], spbs[it])
            if 0 <= step - ATTN_PV_LAG < len(batches):
                atts = {}
                for it in batches[step - ATTN_PV_LAG]:
                    if not blocks[it[0]][3]:
                        afters[it] = afters[it] + carry_ref[it[1], 0:1, :]
                    atts[it] = jnp.exp2(ds.pop(it) + afters[it]).astype(BF16)
                    carry_ref[it[1], 0:1, :] = (afters.pop(it)[0:1, :]
                                                - spbs.pop(it)[0:1, :].astype(F32))
                for it in batches[step - ATTN_PV_LAG]:
                    _, vb, _, first = blocks[it[0]]
                    rows = slice(it[1] * HEAD_DIM, (it[1] + 1) * HEAD_DIM)
                    pv = _mm(vb(lane_groups[it[1]][0]), atts[it])
                    acc_ref[rows, :] = pv if first else acc_ref[rows, :] + pv

    mask = (lax.broadcasted_iota(jnp.int32, (KEY_BLOCK, KEY_BLOCK), 0)
            < lax.broadcasted_iota(jnp.int32, (KEY_BLOCK, KEY_BLOCK), 1) % tq)
    diag = (kd_ref[0], lambda g: vd_ref[0, g, 0], mask, True)

    def interior(j):
        kb = ki_ref[0, pl.ds(pl.multiple_of(j * KEY_BLOCK, KEY_BLOCK), KEY_BLOCK), :]
        return (kb, lambda g: vi_ref[0, g, j], None, False)

    if n_int_static is None:
        pl.when(n_int == 0)(lambda: key_blocks([diag]))
        pl.when(n_int > 0)(lambda: key_blocks([diag, interior(n_int - 1)]))
    elif n_int_static == 0:
        key_blocks([diag])
    else:
        key_blocks([diag, interior(n_int - 1)])

    def any_weight_left():
        worst = carry_ref[0, 0:1, :]
        for i in groups[1:]:
            worst = jnp.maximum(worst, carry_ref[i, 0:1, :])
        return jnp.max(worst) > ZERO_WEIGHT_LOG2

    def block(state):
        i, _ = state
        key_blocks([interior(n_int - 1 - i)])
        return i + 1, any_weight_left()

    lax.while_loop(lambda state: jnp.logical_and(state[0] < n_int, state[1]), block,
                   (jnp.int32(1), any_weight_left()))

    if transpose_out:
        o_ref[0] = acc_ref[...].T.astype(o_ref.dtype)
    else:
        o_ref[0] = acc_ref[...].astype(o_ref.dtype)


def _attention(q, k_int, vt_int, k_diag, vt_diag, *, lane_groups, tq, n_int_static, transpose_out):
    b, q_rows, lanes_all = q.shape
    nq = lanes_all // KEY_BLOCK
    s_int = k_int.shape[1]
    n_blk = vt_int.shape[2]
    width = len(lane_groups) * HEAD_DIM
    assert s_int == n_blk * KEY_BLOCK and k_diag.shape[1] == nq * KEY_BLOCK
    assert vt_diag.shape[2:] == (nq, HEAD_DIM, KEY_BLOCK)
    if transpose_out:
        out_spec = pl.BlockSpec((1, KEY_BLOCK, width), lambda i, t: (i, t, 0))
        out_shape = jax.ShapeDtypeStruct((b, nq * KEY_BLOCK, width), BF16)
    else:
        out_spec = pl.BlockSpec((1, width, KEY_BLOCK), lambda i, t: (i, 0, t))
        out_shape = jax.ShapeDtypeStruct((b, width, nq * KEY_BLOCK), BF16)
    return pl.pallas_call(
        functools.partial(_attn_kernel, lane_groups=lane_groups, tq=tq,
                          n_int_static=n_int_static, transpose_out=transpose_out),
        grid=(b, nq),
        in_specs=[
            pl.BlockSpec((1, q_rows, KEY_BLOCK), lambda i, t: (i, 0, t)),
            pl.BlockSpec((1, s_int, KV_DIM), lambda i, t: (i, 0, 0)),
            pl.BlockSpec((1, N_KV_HEADS, n_blk, HEAD_DIM, KEY_BLOCK), lambda i, t: (i, 0, 0, 0, 0)),
            pl.BlockSpec((1, KEY_BLOCK, KV_DIM), lambda i, t: (i, t, 0)),
            pl.BlockSpec((1, N_KV_HEADS, 1, HEAD_DIM, KEY_BLOCK), lambda i, t: (i, 0, t, 0, 0)),
        ],
        out_specs=out_spec,
        out_shape=out_shape,
        scratch_shapes=[
            pltpu.VMEM((len(lane_groups), KV_DIM, KEY_BLOCK), BF16),
            pltpu.VMEM((KEY_BLOCK, KEY_BLOCK), BF16),
            pltpu.VMEM((len(lane_groups), V7X_SUBLANES, KEY_BLOCK), F32),
            pltpu.VMEM((width, KEY_BLOCK), F32),
        ],
        compiler_params=_compiler_params(("arbitrary", "arbitrary")),
        name="sb_attention",
    )(q, k_int, vt_int, k_diag, vt_diag)


def _trunk(x, conv_prefix, past, w):
    b, t, d = x.shape
    h, conv_state = _layer0(x, conv_prefix, w)
    tq = min(KEY_BLOCK, t)
    k, v, kb, vt, qt = _qkv(h.reshape(b, t, d), w["kv_norm_g"], w["b_norm_g"],
                            w["w_kv"], w["w_vt"], w["w_qt"], tq)
    if past is None:
        assert tq == KEY_BLOCK
        heads = tuple((hh // GROUP, hh) for hh in range(N_HEADS))
        o = _attention(qt, kb, vt, kb, vt, lane_groups=heads, tq=tq,
                       n_int_static=None, transpose_out=True)
        o = o.reshape(b * t, -1)
    else:
        assert t * GROUP == KEY_BLOCK
        k_past, vt_past = past
        q = (qt.reshape(b, N_KV_HEADS, GROUP, HEAD_DIM, t).transpose(0, 1, 3, 2, 4)
             .reshape(b, KV_DIM, KEY_BLOCK))
        pad = KEY_BLOCK - t
        k_new = jnp.pad(kb, ((0, 0), (0, pad), (0, 0)))
        vt_new = jnp.pad(vt, ((0, 0), (0, 0), (0, 0), (0, 0), (0, pad)))
        o = _attention(q, k_past, vt_past, k_new, vt_new,
                       lane_groups=tuple((g, g) for g in range(N_KV_HEADS)), tq=t,
                       n_int_static=k_past.shape[1] // KEY_BLOCK, transpose_out=False)
        o = (o.reshape(b, N_KV_HEADS, HEAD_DIM, GROUP, t).transpose(0, 4, 1, 3, 2)
             .reshape(b * t, -1))
    y = _ffn(h, w["ffn_norm_g"][1], w["ffn_w_gu"], w["ffn_w_down"], 1,
             attn=(o, w["w_o"]), final_g=w["final_norm_g"])
    shape4 = (b, t, N_KV_HEADS, HEAD_DIM)
    return y.reshape(b, t, d), conv_state[None], k.reshape(shape4), v.reshape(shape4)


def kernel(x_prompt, x_sample, state_conv, cache_k, cache_v, a_norm_g, conv_w_in, conv_b_in, conv_w_dw, conv_b_dw, conv_ln_g, conv_ln_b, conv_w_out, conv_b_out, kv_norm_g, w_kv, b_norm_g, w_q, w_o, ffn_norm_g, ffn_w_gu, ffn_w_down, final_norm_g):
    assert a_norm_g.shape[0] == 1 and b_norm_g.shape[0] == 1 and ffn_norm_g.shape[0] == 2
    d = x_prompt.shape[-1]
    row = lambda a: a.reshape(1, -1)
    w = {
        "a_norm_g": row(a_norm_g[0]),
        "conv_w_in": conv_w_in[0].astype(BF16), "conv_b_in": row(conv_b_in[0]),
        "conv_w_dw": conv_w_dw[0], "conv_b_dw": row(conv_b_dw[0]),
        "conv_ln_g": row(conv_ln_g[0]), "conv_ln_b": row(conv_ln_b[0]),
        "conv_w_out": conv_w_out[0].astype(BF16), "conv_b_out": row(conv_b_out[0]),
        "kv_norm_g": row(kv_norm_g), "b_norm_g": row(b_norm_g[0]),
        "w_kv": w_kv.astype(BF16),
        "w_vt": w_kv[:, KV_DIM:].T.astype(BF16),
        "w_qt": w_q[0].T.astype(BF16),
        "w_o": w_o[0].astype(BF16),
        "ffn_norm_g": [row(ffn_norm_g[0]), row(ffn_norm_g[1])],
        "ffn_w_gu": ffn_w_gu.astype(BF16), "ffn_w_down": ffn_w_down.astype(BF16),
        "final_norm_g": row(final_norm_g),
    }

    bp = x_prompt.shape[0]
    zero_prefix = jnp.zeros((bp, CONV_STATE, d), x_prompt.dtype)
    y_p, cs_p, k_p, v_p = _trunk(x_prompt, zero_prefix, None, w)

    bs, past_len = cache_k.shape[0], cache_k.shape[1]
    assert past_len % KEY_BLOCK == 0
    k_past = cache_k.reshape(bs, past_len, KV_DIM).astype(BF16)
    vt_past = (cache_v.reshape(bs, past_len // KEY_BLOCK, KEY_BLOCK, N_KV_HEADS, HEAD_DIM)
               .transpose(0, 3, 1, 4, 2).astype(BF16))
    y_s, cs_s, k_s, v_s = _trunk(x_sample, state_conv[0], (k_past, vt_past), w)
    return (y_p, y_s, cs_p, k_p, v_p, cs_s, k_s, v_s)
```

```python
import functools
import math

import jax
import jax.numpy as jnp
from jax import lax
from jax.experimental import pallas as pl
from jax.experimental.pallas import tpu as pltpu

F32 = jnp.float32
BF16 = jnp.bfloat16

EPS = 1e-6
CONV_WIDTH = 31
CONV_STATE = CONV_WIDTH - 1
N_HEADS = 16
N_KV_HEADS = 4
GROUP = N_HEADS // N_KV_HEADS
HEAD_DIM = 64
KV_DIM = N_KV_HEADS * HEAD_DIM
LOG2E = math.log2(math.e)

V7X_SUBLANES = 8
V7X_LANES = 128
V7X_MXU_DIM = 256
V7X_VMEM_LIMIT_BYTES = 56 * 1024 * 1024

TOKEN_TILE = 512
FFN_TILE = 1024
QKV_TILE = 2048
KEY_BLOCK = V7X_MXU_DIM
CONV_HALO = 32
CONV_ROWS = 64
CONV_LANES = 128
GLU_LANES = 512
FFN_LANES = 256


def _compiler_params(semantics):
    return pltpu.CompilerParams(dimension_semantics=semantics,
                                vmem_limit_bytes=V7X_VMEM_LIMIT_BYTES)


def _resident(shape):
    return pl.BlockSpec(shape, lambda *_: (0,) * len(shape), pipeline_mode=pl.Buffered(1))


def _resident_layer(shape, layer):
    return pl.BlockSpec((None,) + tuple(shape[1:]), lambda *_: (layer, 0, 0),
                        pipeline_mode=pl.Buffered(1))


def _rms(x, g):
    return x * lax.rsqrt(jnp.mean(x * x, axis=-1, keepdims=True) + EPS) * g


def _mm(a, b):
    return jnp.dot(a, b, preferred_element_type=F32)


def _mm_nt(a, b):
    return lax.dot_general(a, b, (((1,), (1,)), ((), ())), preferred_element_type=F32)


def _swish_bf16(x, y=None):
    h = 0.5 * x
    out = h + h * jnp.tanh(h)
    return (out if y is None else out * y).astype(BF16)


def _layer0_kernel(x_ref, pre_ref, ag_ref, win_ref, bin_ref, wdw_ref, bdw_ref,
                   lng_ref, lnb_ref, wout_ref, bout_ref, fg_ref, wgu_ref, wd_ref,
                   h2_ref, st_ref, e_ref, y_ref, h1_ref, xf_ref, mid_ref, *, tt, d, nt, d_ff):
    s = pl.program_id(0)
    t = lax.rem(jnp.minimum(s, pl.num_programs(0) - 2), nt)
    sections = iter(pl.when(s >= -i) for i in range(3))

    @pl.when(s == 0)
    def _():
        h1_ref[...] = jnp.zeros_like(h1_ref)

    first_off = CONV_HALO - CONV_STATE
    last_off = first_off + CONV_WIDTH - 1
    out_rows = min(CONV_ROWS, tt)
    conv_chunks = [(r0, c) for r0 in range(0, tt, out_rows) for c in range(d // CONV_LANES)]

    def conv(chunks):
        for r0, c in chunks:
            lanes = slice(c * CONV_LANES, (c + 1) * CONV_LANES)
            acc = jnp.broadcast_to(bdw_ref[:, lanes], (out_rows, CONV_LANES))
            for r in range(V7X_SUBLANES):
                rows = out_rows + (V7X_SUBLANES if r else 0)
                part = None
                for o in range(r, last_off + 1, V7X_SUBLANES):
                    if o < first_off:
                        continue
                    k = o - first_off
                    lo = r0 + o - r
                    wk = jnp.concatenate([wdw_ref[k * V7X_SUBLANES:(k + 1) * V7X_SUBLANES, lanes]]
                                         * (rows // V7X_SUBLANES), axis=0)
                    term = wk * e_ref[lo:lo + rows, lanes]
                    part = term if part is None else part + term
                acc = acc + (pltpu.roll(part, rows - r, axis=0)[0:out_rows] if r else part)
            y_ref[r0:r0 + out_rows, lanes] = acc

    @next(sections)
    def _():
        xn = _rms(x_ref[0], ag_ref[...]).astype(BF16)
        e_ref[0:CONV_HALO, :] = jnp.where(t == 0, pre_ref[0], e_ref[tt:tt + CONV_HALO, :])
        for lo in range(0, d, GLU_LANES):
            a = _mm(xn, win_ref[:, lo:lo + GLU_LANES]) + bin_ref[:, lo:lo + GLU_LANES]
            g = (_mm(xn, win_ref[:, d + lo:d + lo + GLU_LANES])
                 + bin_ref[:, d + lo:d + lo + GLU_LANES])
            e_ref[CONV_HALO:CONV_HALO + tt, lo:lo + GLU_LANES] = a * (0.5 + 0.5 * jnp.tanh(0.5 * g))
        xf_ref[...] = _rms(h1_ref[...], fg_ref[...]).astype(BF16)

    @next(sections)
    def _():
        for lo in range(0, d_ff, FFN_LANES):
            gate = _mm(xf_ref[...], wgu_ref[:, lo:lo + FFN_LANES])
            up = _mm(xf_ref[...], wgu_ref[:, d_ff + lo:d_ff + lo + FFN_LANES])
            mid_ref[:, lo:lo + FFN_LANES] = _swish_bf16(gate, up)
        conv(conv_chunks)

    @next(sections)
    def _():
        h2_ref[...] = h1_ref[...] + _mm(mid_ref[...], wd_ref[...])
        y = y_ref[...]
        mu = jnp.mean(y, axis=-1, keepdims=True)
        yc = y - mu
        yn = yc * lax.rsqrt(jnp.mean(yc * yc, axis=-1, keepdims=True) + EPS)
        yn = yn * lng_ref[...] + lnb_ref[...]
        h1_ref[...] = x_ref[0] + _mm(_swish_bf16(yn), wout_ref[...]) + bout_ref[...]
        st_ref[0] = e_ref[CONV_HALO + tt - CONV_STATE:CONV_HALO + tt, :]


def _layer0(x, prefix, w):
    b, t, d = x.shape
    tt = min(TOKEN_TILE, t)
    nt = t // tt
    n_tiles = b * nt
    w_gu, w_down = w["ffn_w_gu"], w["ffn_w_down"]
    d_ff = w_down.shape[1]
    assert t % tt == 0 and tt % min(CONV_ROWS, tt) == 0 and tt >= CONV_HALO
    assert d % CONV_LANES == 0 and d % GLU_LANES == 0 and d_ff % FFN_LANES == 0
    halo = jnp.pad(prefix, ((0, 0), (CONV_HALO - CONV_STATE, 0), (0, 0)))
    w_dw8 = jnp.repeat(w["conv_w_dw"], V7X_SUBLANES, axis=0)
    mixer_tile = lambda s: jnp.minimum(s, n_tiles - 1)
    row = lambda n: _resident((1, n))
    return pl.pallas_call(
        functools.partial(_layer0_kernel, tt=tt, d=d, nt=nt, d_ff=d_ff),
        grid=(n_tiles + 1,),
        in_specs=[
            pl.BlockSpec((1, tt, d), lambda s: (mixer_tile(s) // nt, mixer_tile(s) % nt, 0)),
            pl.BlockSpec((1, CONV_HALO, d), lambda s: (mixer_tile(s) // nt, 0, 0)),
            row(d), _resident((d, 2 * d)), row(2 * d), _resident(w_dw8.shape), row(d),
            row(d), row(d), _resident((d, d)), row(d),
            row(d), _resident_layer(w_gu.shape, 0), _resident_layer(w_down.shape, 0),
        ],
        out_specs=[
            pl.BlockSpec((tt, d), lambda s: (jnp.maximum(s - 1, 0), 0)),
            pl.BlockSpec((1, CONV_STATE, d), lambda s: (mixer_tile(s) // nt, 0, 0)),
        ],
        out_shape=[jax.ShapeDtypeStruct((b * t, d), F32),
                   jax.ShapeDtypeStruct((b, CONV_STATE, d), F32)],
        scratch_shapes=[
            pltpu.VMEM((CONV_HALO + tt, d), F32),
            pltpu.VMEM((tt, d), F32),
            pltpu.VMEM((tt, d), F32),
            pltpu.VMEM((tt, d), BF16),
            pltpu.VMEM((tt, d_ff), BF16),
        ],
        compiler_params=_compiler_params(("arbitrary",)),
        name="layer0",
    )(x, halo, w["a_norm_g"], w["conv_w_in"], w["conv_b_in"], w_dw8, w["conv_b_dw"],
      w["conv_ln_g"], w["conv_ln_b"], w["conv_w_out"], w["conv_b_out"],
      w["ffn_norm_g"][0], w_gu, w_down)


def _ffn_kernel(h_ref, o_ref, wo_ref, g_ref, wgu_ref, wd_ref, fg_ref, out_ref, xf_ref, mid_ref,
                *, d_ff):
    out_ref[...] = h_ref[...] + _mm(o_ref[...], wo_ref[...])
    xf_ref[...] = _rms(out_ref[...], g_ref[...]).astype(BF16)
    for lo in range(0, d_ff, FFN_LANES):
        gate = _mm(xf_ref[...], wgu_ref[:, lo:lo + FFN_LANES])
        up = _mm(xf_ref[...], wgu_ref[:, d_ff + lo:d_ff + lo + FFN_LANES])
        mid_ref[:, lo:lo + FFN_LANES] = _swish_bf16(gate, up)
    out_ref[...] = _rms(out_ref[...] + _mm(mid_ref[...], wd_ref[...]), fg_ref[...])


def _ffn(h, o, w_o, norm_g, w_gu, w_down, layer, final_g):
    n, d = h.shape
    d_ff = w_down.shape[1]
    tm = min(FFN_TILE, n)
    assert n % tm == 0 and d_ff % FFN_LANES == 0
    tile = lambda width: pl.BlockSpec((tm, width), lambda i: (i, 0))
    return pl.pallas_call(
        functools.partial(_ffn_kernel, d_ff=d_ff),
        grid=(n // tm,),
        in_specs=[tile(d), tile(o.shape[1]), _resident(w_o.shape), _resident((1, d)),
                  _resident_layer(w_gu.shape, layer), _resident_layer(w_down.shape, layer),
                  _resident((1, d))],
        out_specs=tile(d),
        out_shape=jax.ShapeDtypeStruct((n, d), F32),
        scratch_shapes=[pltpu.VMEM((tm, d), BF16),
                        pltpu.VMEM((tm, d_ff), BF16)],
        compiler_params=_compiler_params(("arbitrary",)),
        name="ffn_attn",
    )(h, o, w_o, norm_g, w_gu, w_down, final_g)


def _qkv_kernel(h_ref, kvg_ref, bg_ref, wkv_ref, wvt_ref, wqt_ref,
                k_ref, v_ref, kb_ref, vt_ref, qt_ref, *, kb):
    h = h_ref[0]
    hn = h * lax.rsqrt(jnp.mean(h * h, axis=-1, keepdims=True) + EPS)
    xkv = (hn * kvg_ref[...]).astype(BF16)
    xq = (hn * bg_ref[...]).astype(BF16)
    kv = _mm(xkv, wkv_ref[...])
    k = kv[:, :KV_DIM]
    k_ref[0] = k
    v_ref[0] = kv[:, KV_DIM:]
    kb_ref[0] = k.astype(BF16)
    if h.shape[0] % V7X_LANES == 0:
        vt = kv[:, KV_DIM:].T.astype(BF16)
    else:
        vt = _mm_nt(wvt_ref[...], xkv).astype(BF16)
    for g in range(N_KV_HEADS):
        for j in range(vt.shape[1] // kb):
            vt_ref[0, g, j] = vt[g * HEAD_DIM:(g + 1) * HEAD_DIM, j * kb:(j + 1) * kb]
    qt_ref[0] = (_mm_nt(wqt_ref[...], xq) * (HEAD_DIM ** -0.5 * LOG2E)).astype(BF16)


def _qkv(h, kv_norm_g, b_norm_g, w_kv, w_vt, w_qt, kb):
    b, t, d = h.shape
    tm = min(QKV_TILE, t)
    assert t % tm == 0 and tm % kb == 0
    hd = w_qt.shape[0]
    return pl.pallas_call(
        functools.partial(_qkv_kernel, kb=kb),
        grid=(b, t // tm),
        in_specs=[
            pl.BlockSpec((1, tm, d), lambda i, j: (i, j, 0)),
            _resident((1, d)), _resident((1, d)),
            _resident(w_kv.shape), _resident(w_vt.shape), _resident(w_qt.shape),
        ],
        out_specs=[
            pl.BlockSpec((1, tm, KV_DIM), lambda i, j: (i, j, 0)),
            pl.BlockSpec((1, tm, KV_DIM), lambda i, j: (i, j, 0)),
            pl.BlockSpec((1, tm, KV_DIM), lambda i, j: (i, j, 0)),
            pl.BlockSpec((1, N_KV_HEADS, tm // kb, HEAD_DIM, kb), lambda i, j: (i, 0, j, 0, 0)),
            pl.BlockSpec((1, hd, tm), lambda i, j: (i, 0, j)),
        ],
        out_shape=[
            jax.ShapeDtypeStruct((b, t, KV_DIM), F32),
            jax.ShapeDtypeStruct((b, t, KV_DIM), F32),
            jax.ShapeDtypeStruct((b, t, KV_DIM), BF16),
            jax.ShapeDtypeStruct((b, N_KV_HEADS, t // kb, HEAD_DIM, kb), BF16),
            jax.ShapeDtypeStruct((b, hd, t), BF16),
        ],
        compiler_params=_compiler_params(("arbitrary", "arbitrary")),
        name="qkv",
    )(h, kv_norm_g, b_norm_g, w_kv, w_vt, w_qt)


ATTN_DOT_BATCH = 8
ATTN_TRI_LAG = 1
ATTN_PV_LAG = 2
EXP2_CLAMP = 126.0
MASKED_LOGIT = -1e30
ZERO_WEIGHT_LOG2 = -160.0


def _softplus2(z):
    return jnp.maximum(z, jnp.log(1.0 + jnp.exp2(jnp.minimum(z, EXP2_CLAMP))) * LOG2E)


def _attn_kernel(q_ref, ki_ref, vi_ref, kd_ref, vd_ref, o_ref,
                 qz_ref, tri_ref, carry_ref, acc_ref,
                 *, lane_groups, tq, n_int_static, transpose_out):
    n_int = pl.program_id(1) if n_int_static is None else n_int_static
    groups = range(len(lane_groups))

    @pl.when(jnp.logical_and(pl.program_id(0) == 0, pl.program_id(1) == 0))
    def _():
        qz_ref[...] = jnp.zeros_like(qz_ref)
        si = lax.broadcasted_iota(jnp.int32, (KEY_BLOCK, KEY_BLOCK), 0)
        sj = lax.broadcasted_iota(jnp.int32, (KEY_BLOCK, KEY_BLOCK), 1)
        tri_ref[...] = jnp.where(sj > si, -1.0, 0.0).astype(BF16)

    for i, (g, qb) in enumerate(lane_groups):
        qz_ref[i, g * HEAD_DIM:(g + 1) * HEAD_DIM, :] = q_ref[0, qb * HEAD_DIM:(qb + 1) * HEAD_DIM, :]

    def key_blocks(blocks):
        items = [(blk, i) for blk in range(len(blocks)) for i in groups]
        per = min(ATTN_DOT_BATCH, len(lane_groups))
        assert len(lane_groups) % per == 0
        batches = [items[p:p + per] for p in range(0, len(items), per)]
        zs, ds, spbs, afters = {}, {}, {}, {}
        for step in range(len(batches) + ATTN_PV_LAG):
            if step < len(batches):
                for it in batches[step]:
                    kb, _, _, _, lanes = blocks[it[0]]
                    zs[it] = _mm(kb, qz_ref[it[1], :, lanes])
            if 0 <= step - ATTN_TRI_LAG < len(batches):
                for it in batches[step - ATTN_TRI_LAG]:
                    mask = blocks[it[0]][2]
                    z = zs.pop(it)
                    sp = _softplus2(z)
                    if mask is None:
                        ds[it] = z - sp
                    else:
                        sp = jnp.where(mask, sp, 0.0)
                        ds[it] = jnp.where(mask, z - sp, MASKED_LOGIT)
                    spbs[it] = sp.astype(BF16)
                for it in batches[step - ATTN_TRI_LAG]:
                    keys = spbs[it].shape[0]
                    afters[it] = _mm(tri_ref[0:keys, 0:keys], spbs[it])
            if 0 <= step - ATTN_PV_LAG < len(batches):
                atts = {}
                for it in batches[step - ATTN_PV_LAG]:
                    _, _, _, first, lanes = blocks[it[0]]
                    if first:
                        carry_ref[it[1], 0:1, :] = jnp.zeros((1, KEY_BLOCK), F32)
                    else:
                        afters[it] = afters[it] + carry_ref[it[1], 0:1, lanes]
                    atts[it] = jnp.exp2(ds.pop(it) + afters[it]).astype(BF16)
                    carry_ref[it[1], 0:1, lanes] = (afters.pop(it)[0:1, :]
                                                    - spbs.pop(it)[0:1, :].astype(F32))
                for it in batches[step - ATTN_PV_LAG]:
                    _, vb, _, first, lanes = blocks[it[0]]
                    rows = slice(it[1] * HEAD_DIM, (it[1] + 1) * HEAD_DIM)
                    pv = _mm(vb(lane_groups[it[1]][0]), atts[it])
                    if first:
                        acc_ref[rows, :] = jnp.zeros((HEAD_DIM, KEY_BLOCK), F32)
                        acc_ref[rows, lanes] = pv
                    else:
                        acc_ref[rows, lanes] = acc_ref[rows, lanes] + pv

    every = slice(0, KEY_BLOCK)
    half = KEY_BLOCK // 2

    def causal(keys, lanes):
        return (lax.broadcasted_iota(jnp.int32, (keys, lanes), 0)
                < lax.broadcasted_iota(jnp.int32, (keys, lanes), 1) % tq)

    if tq == KEY_BLOCK:
        diag = [(kd_ref[0, half:, :], lambda g: vd_ref[0, g, 0, :, half:], causal(half, half),
                 True, slice(half, KEY_BLOCK)),
                (kd_ref[0, :half, :], lambda g: vd_ref[0, g, 0, :, :half], causal(half, KEY_BLOCK),
                 False, every)]
    else:
        diag = [(kd_ref[0], lambda g: vd_ref[0, g, 0], causal(KEY_BLOCK, KEY_BLOCK), True, every)]

    def interior(j):
        kb = ki_ref[0, pl.ds(pl.multiple_of(j * KEY_BLOCK, KEY_BLOCK), KEY_BLOCK), :]
        return (kb, lambda g: vi_ref[0, g, j], None, False, every)

    if n_int_static is None:
        pl.when(n_int == 0)(lambda: key_blocks(diag))
        pl.when(n_int > 0)(lambda: key_blocks(diag + [interior(n_int - 1)]))
    elif n_int_static == 0:
        key_blocks(diag)
    else:
        key_blocks(diag + [interior(n_int - 1)])

    def any_weight_left():
        worst = carry_ref[0, 0:1, :]
        for i in groups[1:]:
            worst = jnp.maximum(worst, carry_ref[i, 0:1, :])
        return jnp.max(worst) > ZERO_WEIGHT_LOG2

    def block(state):
        i, _ = state
        key_blocks([interior(n_int - 1 - i)])
        return i + 1, any_weight_left()

    lax.while_loop(lambda state: jnp.logical_and(state[0] < n_int, state[1]), block,
                   (jnp.int32(1), any_weight_left()))

    if transpose_out:
        o_ref[0] = acc_ref[...].T.astype(o_ref.dtype)
    else:
        o_ref[0] = acc_ref[...].astype(o_ref.dtype)


def _attention(q, k_int, vt_int, k_diag, vt_diag, *, lane_groups, tq, n_int_static, transpose_out):
    b, q_rows, lanes_all = q.shape
    nq = lanes_all // KEY_BLOCK
    s_int = k_int.shape[1]
    n_blk = vt_int.shape[2]
    width = len(lane_groups) * HEAD_DIM
    assert s_int == n_blk * KEY_BLOCK and k_diag.shape[1] == nq * KEY_BLOCK
    assert vt_diag.shape[2:] == (nq, HEAD_DIM, KEY_BLOCK)
    if transpose_out:
        out_spec = pl.BlockSpec((1, KEY_BLOCK, width), lambda i, t: (i, t, 0))
        out_shape = jax.ShapeDtypeStruct((b, nq * KEY_BLOCK, width), BF16)
    else:
        out_spec = pl.BlockSpec((1, width, KEY_BLOCK), lambda i, t: (i, 0, t))
        out_shape = jax.ShapeDtypeStruct((b, width, nq * KEY_BLOCK), BF16)
    return pl.pallas_call(
        functools.partial(_attn_kernel, lane_groups=lane_groups, tq=tq,
                          n_int_static=n_int_static, transpose_out=transpose_out),
        grid=(b, nq),
        in_specs=[
            pl.BlockSpec((1, q_rows, KEY_BLOCK), lambda i, t: (i, 0, t)),
            pl.BlockSpec((1, s_int, KV_DIM), lambda i, t: (i, 0, 0)),
            pl.BlockSpec((1, N_KV_HEADS, n_blk, HEAD_DIM, KEY_BLOCK), lambda i, t: (i, 0, 0, 0, 0)),
            pl.BlockSpec((1, KEY_BLOCK, KV_DIM), lambda i, t: (i, t, 0)),
            pl.BlockSpec((1, N_KV_HEADS, 1, HEAD_DIM, KEY_BLOCK), lambda i, t: (i, 0, t, 0, 0)),
        ],
        out_specs=out_spec,
        out_shape=out_shape,
        scratch_shapes=[
            pltpu.VMEM((len(lane_groups), KV_DIM, KEY_BLOCK), BF16),
            pltpu.VMEM((KEY_BLOCK, KEY_BLOCK), BF16),
            pltpu.VMEM((len(lane_groups), V7X_SUBLANES, KEY_BLOCK), F32),
            pltpu.VMEM((width, KEY_BLOCK), F32),
        ],
        compiler_params=_compiler_params(("arbitrary", "arbitrary")),
        name="sb_attention",
    )(q, k_int, vt_int, k_diag, vt_diag)


def _trunk(x, conv_prefix, past, w):
    b, t, d = x.shape
    h, conv_state = _layer0(x, conv_prefix, w)
    tq = min(KEY_BLOCK, t)
    k, v, kb, vt, qt = _qkv(h.reshape(b, t, d), w["kv_norm_g"], w["b_norm_g"],
                            w["w_kv"], w["w_vt"], w["w_qt"], tq)
    if past is None:
        assert tq == KEY_BLOCK
        heads = tuple((hh // GROUP, hh) for hh in range(N_HEADS))
        o = _attention(qt, kb, vt, kb, vt, lane_groups=heads, tq=tq,
                       n_int_static=None, transpose_out=True)
        o = o.reshape(b * t, -1)
    else:
        assert t * GROUP == KEY_BLOCK
        k_past, vt_past = past
        q = (qt.reshape(b, N_KV_HEADS, GROUP, HEAD_DIM, t).transpose(0, 1, 3, 2, 4)
             .reshape(b, KV_DIM, KEY_BLOCK))
        pad = KEY_BLOCK - t
        k_new = jnp.pad(kb, ((0, 0), (0, pad), (0, 0)))
        vt_new = jnp.pad(vt, ((0, 0), (0, 0), (0, 0), (0, 0), (0, pad)))
        o = _attention(q, k_past, vt_past, k_new, vt_new,
                       lane_groups=tuple((g, g) for g in range(N_KV_HEADS)), tq=t,
                       n_int_static=k_past.shape[1] // KEY_BLOCK, transpose_out=False)
        o = (o.reshape(b, N_KV_HEADS, HEAD_DIM, GROUP, t).transpose(0, 4, 1, 3, 2)
             .reshape(b * t, -1))
    y = _ffn(h, o, w["w_o"], w["ffn_norm_g"][1], w["ffn_w_gu"], w["ffn_w_down"], 1,
             w["final_norm_g"])
    shape4 = (b, t, N_KV_HEADS, HEAD_DIM)
    return y.reshape(b, t, d), conv_state[None], k.reshape(shape4), v.reshape(shape4)


def kernel(x_prompt, x_sample, state_conv, cache_k, cache_v, a_norm_g, conv_w_in, conv_b_in, conv_w_dw, conv_b_dw, conv_ln_g, conv_ln_b, conv_w_out, conv_b_out, kv_norm_g, w_kv, b_norm_g, w_q, w_o, ffn_norm_g, ffn_w_gu, ffn_w_down, final_norm_g):
    assert a_norm_g.shape[0] == 1 and b_norm_g.shape[0] == 1 and ffn_norm_g.shape[0] == 2
    d = x_prompt.shape[-1]
    row = lambda a: a.reshape(1, -1)
    w = {
        "a_norm_g": row(a_norm_g[0]),
        "conv_w_in": conv_w_in[0].astype(BF16), "conv_b_in": row(conv_b_in[0]),
        "conv_w_dw": conv_w_dw[0], "conv_b_dw": row(conv_b_dw[0]),
        "conv_ln_g": row(conv_ln_g[0]), "conv_ln_b": row(conv_ln_b[0]),
        "conv_w_out": conv_w_out[0].astype(BF16), "conv_b_out": row(conv_b_out[0]),
        "kv_norm_g": row(kv_norm_g), "b_norm_g": row(b_norm_g[0]),
        "w_kv": w_kv.astype(BF16),
        "w_vt": w_kv[:, KV_DIM:].T.astype(BF16),
        "w_qt": w_q[0].T.astype(BF16),
        "w_o": w_o[0].astype(BF16),
        "ffn_norm_g": [row(ffn_norm_g[0]), row(ffn_norm_g[1])],
        "ffn_w_gu": ffn_w_gu.astype(BF16), "ffn_w_down": ffn_w_down.astype(BF16),
        "final_norm_g": row(final_norm_g),
    }

    bp = x_prompt.shape[0]
    zero_prefix = jnp.zeros((bp, CONV_STATE, d), x_prompt.dtype)
    y_p, cs_p, k_p, v_p = _trunk(x_prompt, zero_prefix, None, w)

    bs, past_len = cache_k.shape[0], cache_k.shape[1]
    assert past_len % KEY_BLOCK == 0
    k_past = cache_k.reshape(bs, past_len, KV_DIM).astype(BF16)
    vt_past = (cache_v.reshape(bs, past_len // KEY_BLOCK, KEY_BLOCK, N_KV_HEADS, HEAD_DIM)
               .transpose(0, 3, 1, 4, 2).astype(BF16))
    y_s, cs_s, k_s, v_s = _trunk(x_sample, state_conv[0], (k_past, vt_past), w)
    return (y_p, y_s, cs_p, k_p, v_p, cs_s, k_s, v_s)
```
